```python
import math
import jax, jax.numpy as jnp
from jax import lax
import numpy as np

D_MODEL = 1024
BATCH = 4
SEQ = 4096
DEPTH = 4

D_MIX = D_MODEL
GROUP = D_MIX // 4
HEAD_DIM = 64
N_HEADS = GROUP // HEAD_DIM
D_IN = 10 * GROUP
CONV_WIDTH = 31
RET_CHUNK = 128
RET_ROT_THETA = 10000.0
MOBA_BLOCK = 256
MOBA_TOPK = 3
MOBA_QCHUNK = 64
ROPE_THETA = 500000.0
ROT_DIM = HEAD_DIM // 4
POOL_WINDOWS = (2, 4, 8, 16)
POOL_GROUP = GROUP // len(POOL_WINDOWS)
D_FF = -(-8 * D_MODEL // (3 * 256)) * 256
NEG = -1e30

kernel_name = "hybrid_parallel_conv_ret_moba_pool"


def rms_norm(x, g, eps=1e-6):
    xf = x.astype(jnp.float32)
    y = xf * lax.rsqrt(jnp.mean(xf * xf, axis=-1, keepdims=True) + eps)
    return (y * g.astype(jnp.float32)).astype(x.dtype)


def apply_rotary(x, rot_dim, theta):
    s = x.shape[1]
    pos = jnp.arange(s, dtype=jnp.float32)
    inv = 1.0 / (theta ** (jnp.arange(0, rot_dim, 2, dtype=jnp.float32) / rot_dim))
    ang = pos[:, None] * inv[None, :]
    cos = jnp.cos(ang)[None, :, None, :]
    sin = jnp.sin(ang)[None, :, None, :]
    xf = x.astype(jnp.float32)
    half = rot_dim // 2
    x1 = xf[..., :half]
    x2 = xf[..., half:rot_dim]
    out = jnp.concatenate([x1 * cos - x2 * sin, x1 * sin + x2 * cos, xf[..., rot_dim:]], axis=-1)
    return out.astype(x.dtype)


def conv_mixer(u, w_dw, b_dw, ln_g, ln_b):
    a, gate = jnp.split(u, 2, axis=-1)
    h = a * jax.nn.sigmoid(gate)
    h = lax.conv_general_dilated(
        h, w_dw[:, None, :].astype(h.dtype), window_strides=(1,),
        padding=[(CONV_WIDTH - 1, 0)],
        dimension_numbers=('NWC', 'WIO', 'NWC'),
        feature_group_count=GROUP) + b_dw
    hf = h.astype(jnp.float32)
    mu = jnp.mean(hf, axis=-1, keepdims=True)
    var = jnp.mean(jnp.square(hf - mu), axis=-1, keepdims=True)
    hf = (hf - mu) * lax.rsqrt(var + 1e-5) * ln_g + ln_b
    return jax.nn.silu(hf).astype(u.dtype)


def retention(q, k, v, g, gn_g, gn_b):
    b, s, _ = q.shape
    dt = q.dtype
    q = apply_rotary(q.reshape(b, s, N_HEADS, HEAD_DIM), HEAD_DIM, RET_ROT_THETA).astype(jnp.float32)
    k = apply_rotary(k.reshape(b, s, N_HEADS, HEAD_DIM), HEAD_DIM, RET_ROT_THETA).astype(jnp.float32)
    k = k * (HEAD_DIM ** -0.5)
    v = v.reshape(b, s, N_HEADS, HEAD_DIM).astype(jnp.float32)
    c = RET_CHUNK
    nc = s // c

    def to_chunks(t):
        return t.reshape(b, nc, c, N_HEADS, HEAD_DIM).transpose(1, 0, 3, 2, 4)

    log_gamma = jnp.log(1.0 - 2.0 ** (-5.0 - jnp.arange(N_HEADS, dtype=jnp.float32)))
    idx = jnp.arange(c, dtype=jnp.float32)
    diff = idx[:, None] - idx[None, :]
    decay_mask = jnp.where(diff[None] >= 0,
                           jnp.exp(jnp.maximum(diff, 0.0)[None] * log_gamma[:, None, None]), 0.0)
    q_decay = jnp.exp((idx + 1.0)[None, :] * log_gamma[:, None])
    k_decay = jnp.exp((c - 1.0 - idx)[None, :] * log_gamma[:, None])
    chunk_decay = jnp.exp(c * log_gamma)

    def step(state, xs):
        qc, kc, vc = xs
        scores = jnp.einsum('bhid,bhjd->bhij', qc, kc) * decay_mask
        o = jnp.einsum('bhij,bhjd->bhid', scores, vc)
        o = o + jnp.einsum('bhid,bhde->bhie', qc * q_decay[:, :, None], state)
        state = state * chunk_decay[:, None, None] + jnp.einsum(
            'bhjd,bhje->bhde', kc * k_decay[:, :, None], vc)
        return state, o

    state0 = jnp.zeros((b, N_HEADS, HEAD_DIM, HEAD_DIM), jnp.float32)
    _, o = lax.scan(step, state0, (to_chunks(q), to_chunks(k), to_chunks(v)))
    o = o.transpose(1, 0, 3, 2, 4).reshape(b, s, N_HEADS, HEAD_DIM)
    mu = jnp.mean(o, axis=-1, keepdims=True)
    var = jnp.mean(jnp.square(o - mu), axis=-1, keepdims=True)
    o = ((o - mu) * lax.rsqrt(var + 1e-5)).reshape(b, s, GROUP) * gn_g + gn_b
    return (jax.nn.silu(g.astype(jnp.float32)) * o).astype(dt)


def moba_attention(q, k, v):
    b, s, _ = q.shape
    dt = q.dtype
    q = apply_rotary(q.reshape(b, s, N_HEADS, HEAD_DIM), ROT_DIM, ROPE_THETA)
    k = apply_rotary(k.reshape(b, s, N_HEADS, HEAD_DIM), ROT_DIM, ROPE_THETA)
    v = v.reshape(b, s, N_HEADS, HEAD_DIM)
    nb = -(-s // MOBA_BLOCK)
    sp = nb * MOBA_BLOCK
    if sp != s:
        pad = ((0, 0), (0, sp - s), (0, 0), (0, 0))
        q, k, v = jnp.pad(q, pad), jnp.pad(k, pad), jnp.pad(v, pad)
    bh = b * N_HEADS
    q = q.transpose(0, 2, 1, 3).reshape(bh, sp, HEAD_DIM)
    kb = k.transpose(0, 2, 1, 3).reshape(bh, nb, MOBA_BLOCK, HEAD_DIM)
    vb = v.transpose(0, 2, 1, 3).reshape(bh, nb, MOBA_BLOCK, HEAD_DIM)
    k_mean = jnp.mean(kb.astype(jnp.float32), axis=2)
    topk = min(MOBA_TOPK, nb)
    nq = sp // MOBA_QCHUNK
    q_chunks = q.reshape(bh, nq, MOBA_QCHUNK, HEAD_DIM).transpose(1, 0, 2, 3)
    scale = HEAD_DIM ** -0.5
    bh_idx = jnp.arange(bh)[:, None, None]

    def attend(args):
        qc, ci = args
        start = ci * MOBA_QCHUNK
        n = start // MOBA_BLOCK
        qpos = start + jnp.arange(MOBA_QCHUNK)
        gate = jnp.einsum('zqd,znd->zqn', qc.astype(jnp.float32), k_mean)
        gate = jnp.where((jnp.arange(nb) < n)[None, None, :], gate, NEG)
        _, sel = lax.top_k(gate, topk)
        sel_valid = jnp.arange(topk) < n
        kg = kb[bh_idx, sel]
        vg = vb[bh_idx, sel]
        s_sel = jnp.einsum('zqd,zqtkd->zqtk', qc, kg).astype(jnp.float32) * scale
        s_sel = jnp.where(sel_valid[None, None, :, None], s_sel, NEG)
        s_sel = s_sel.reshape(bh, MOBA_QCHUNK, topk * MOBA_BLOCK)
        k_own = lax.dynamic_index_in_dim(kb, n, axis=1, keepdims=False)
        v_own = lax.dynamic_index_in_dim(vb, n, axis=1, keepdims=False)
        s_own = jnp.einsum('zqd,zkd->zqk', qc, k_own).astype(jnp.float32) * scale
        kpos = n * MOBA_BLOCK + jnp.arange(MOBA_BLOCK)
        s_own = jnp.where(kpos[None, None, :] <= qpos[None, :, None], s_own, NEG)
        p = jax.nn.softmax(jnp.concatenate([s_sel, s_own], axis=-1), axis=-1)
        p_sel = p[..., :topk * MOBA_BLOCK].reshape(bh, MOBA_QCHUNK, topk, MOBA_BLOCK)
        p_own = p[..., topk * MOBA_BLOCK:]
        o = (jnp.einsum('zqtk,zqtkd->zqd', p_sel.astype(vg.dtype), vg)
             + jnp.einsum('zqk,zkd->zqd', p_own.astype(v_own.dtype), v_own))
        return o.astype(dt)

    o = lax.map(attend, (q_chunks, jnp.arange(nq)))
    o = o.transpose(1, 0, 2, 3).reshape(b, N_HEADS, sp, HEAD_DIM)[:, :, :s]
    return o.transpose(0, 2, 1, 3).reshape(b, s, GROUP)


def pool_mixer(u, w_pool, b_pool, pool_scale):
    b, s, _ = u.shape
    uf = u.astype(jnp.float32)
    cs = jnp.concatenate([jnp.zeros((b, 1, GROUP), jnp.float32), jnp.cumsum(uf, axis=1)], axis=1)
    t = jnp.arange(s)
    outs = []
    for gi, w in enumerate(POOL_WINDOWS):
        sl = slice(gi * POOL_GROUP, (gi + 1) * POOL_GROUP)
        lo = jnp.maximum(t + 1 - w, 0)
        cnt = (t + 1 - lo).astype(jnp.float32)[None, :, None]
        mean = (cs[:, 1:, sl] - cs[:, lo, sl]) / cnt
        outs.append(mean - uf[:, :, sl])
    pooled = jnp.stack(outs, axis=2)
    y = jnp.einsum('bsgc,gcd->bsgd', pooled, w_pool.astype(jnp.float32)).reshape(b, s, GROUP)
    return ((y + b_pool) * pool_scale).astype(u.dtype)


def setup_inputs(seed: int = 0) -> dict:
    key = jax.random.key(seed)
    ks = jax.random.split(key, 24)
    f32 = jnp.float32
    nrm = lambda k, shp, sc: jax.random.normal(k, shp, f32) * sc
    L = DEPTH
    return {
        "x": nrm(ks[0], (BATCH, SEQ, D_MODEL), 1.0),
        "attn_pre_g": 1.0 + nrm(ks[1], (L, D_MODEL), 0.05),
        "w_in": nrm(ks[2], (L, D_MODEL, D_IN), D_MODEL ** -0.5),
        "conv_w": nrm(ks[3], (L, CONV_WIDTH, GROUP), CONV_WIDTH ** -0.5),
        "conv_b": nrm(ks[4], (L, GROUP), 0.02),
        "conv_ln_g": 1.0 + nrm(ks[5], (L, GROUP), 0.05),
        "conv_ln_b": nrm(ks[6], (L, GROUP), 0.02),
        "ret_gn_g": 1.0 + nrm(ks[7], (L, GROUP), 0.05),
        "ret_gn_b": nrm(ks[8], (L, GROUP), 0.02),
        "pool_w": nrm(ks[9], (L, len(POOL_WINDOWS), POOL_GROUP, POOL_GROUP), POOL_GROUP ** -0.5),
        "pool_b": nrm(ks[10], (L, GROUP), 0.02),
        "pool_scale": 1.0 + nrm(ks[11], (L, GROUP), 0.1),
        "w_out": nrm(ks[12], (L, D_MIX, D_MODEL), D_MIX ** -0.5),
        "attn_post_g": 1.0 + nrm(ks[13], (L, D_MODEL), 0.05),
        "ffn_pre_g": 1.0 + nrm(ks[14], (L, D_MODEL), 0.05),
        "w_gate": nrm(ks[15], (L, D_MODEL, D_FF), D_MODEL ** -0.5),
        "w_up": nrm(ks[16], (L, D_MODEL, D_FF), D_MODEL ** -0.5),
        "w_down": nrm(ks[17], (L, D_FF, D_MODEL), D_FF ** -0.5),
        "ffn_post_g": 1.0 + nrm(ks[18], (L, D_MODEL), 0.05),
    }


def reference(x, attn_pre_g, w_in, conv_w, conv_b, conv_ln_g, conv_ln_b, ret_gn_g, ret_gn_b,
              pool_w, pool_b, pool_scale, w_out, attn_post_g, ffn_pre_g, w_gate, w_up, w_down,
              ffn_post_g):
    G = GROUP
    for l in range(DEPTH):
        h = rms_norm(x, attn_pre_g[l])
        u = jnp.einsum('bsd,de->bse', h, w_in[l])
        u_conv, rq, rk, rv, rg, mq, mk, mv, u_pool = jnp.split(
            u, [2 * G, 3 * G, 4 * G, 5 * G, 6 * G, 7 * G, 8 * G, 9 * G], axis=-1)
        y_a = conv_mixer(u_conv, conv_w[l], conv_b[l], conv_ln_g[l], conv_ln_b[l])
        y_b = retention(rq, rk, rv, rg, ret_gn_g[l], ret_gn_b[l])
        y_c = moba_attention(mq, mk, mv)
        y_d = pool_mixer(u_pool, pool_w[l], pool_b[l], pool_scale[l])
        y = jnp.concatenate([y_a, y_b, y_c, y_d], axis=-1)
        y = jnp.einsum('bse,ed->bsd', y, w_out[l])
        x = x + rms_norm(y, attn_post_g[l])
        h = rms_norm(x, ffn_pre_g[l])
        f = jax.nn.silu(jnp.einsum('bsd,df->bsf', h, w_gate[l])) * jnp.einsum('bsd,df->bsf', h, w_up[l])
        f = jnp.einsum('bsf,fd->bsd', f, w_down[l])
        x = x + rms_norm(f, ffn_post_g[l])
    return x
```

```python
import functools
import math

import jax
import jax.numpy as jnp
from jax import lax
from jax.experimental import pallas as pl
from jax.experimental.pallas import tpu as pltpu

F32 = jnp.float32
BF16 = jnp.bfloat16

GROUP = 256
HEAD_DIM = 64
N_HEADS = GROUP // HEAD_DIM
CONV_WIDTH = 31
RET_ROT_THETA = 10000.0
MOBA_BLOCK = 256
MOBA_TOPK = 3
ROPE_THETA = 500000.0
ROT_DIM = HEAD_DIM // 4
POOL_WINDOWS = (2, 4, 8, 16)
NEG = -1e30
BIG = 1e30

ROW_TILE = 512
LOCAL_TILE = 512
LOCAL_CHUNK = 64
CONV_HALO = 32
POOL_HALO = 16
RET_CHUNK = 256
FF_CHUNK = 256
VMEM_LIMIT = 56 * 1024 * 1024


def _cparams(sem):
    return pltpu.CompilerParams(dimension_semantics=sem, vmem_limit_bytes=VMEM_LIMIT)


def _const_spec(shape):
    nd = len(shape)
    return pl.BlockSpec(shape, lambda *_: (0,) * nd, pipeline_mode=pl.Buffered(1))


def _rms(x, g):
    return x * lax.rsqrt(jnp.mean(x * x, axis=-1, keepdims=True) + 1e-6) * g


def _silu(x):
    return x * jax.nn.sigmoid(x)


def _lane_head(shape, axis):
    return lax.broadcasted_iota(jnp.int32, shape, axis) // HEAD_DIM


def _rotate_half(x, cos, sin_signed, half):
    n = x.shape[-1]
    lane = lax.broadcasted_iota(jnp.int32, x.shape, 1) % HEAD_DIM
    partner = jnp.where(lane < half, pltpu.roll(x, n - half, 1), pltpu.roll(x, half, 1))
    return x * cos + partner * sin_signed


def _in_proj_kernel(x_ref, g_ref, w_ref, cr_ref, sr_ref, cm_ref, sm_ref,
                    hc_ref, rq_ref, rk_ref, rv_ref, rg_ref, mqt_ref, mk_ref, mvt_ref, km_ref, up_ref):
    G = GROUP
    hb = _rms(x_ref[...], g_ref[...]).astype(BF16)

    def proj(i, n=1):
        return jnp.dot(hb, w_ref[:, i * G:(i + n) * G], preferred_element_type=F32)

    uc = proj(0, 2)
    hc_ref[...] = uc[:, :G] * jax.nn.sigmoid(uc[:, G:])
    cr, sr = cr_ref[...], sr_ref[...]
    rq_ref[...] = _rotate_half(proj(2), cr, sr, HEAD_DIM // 2).astype(BF16)
    rk_ref[...] = (_rotate_half(proj(3), cr, sr, HEAD_DIM // 2) * (HEAD_DIM ** -0.5)).astype(BF16)
    rv_ref[...] = proj(4).astype(BF16)
    rg_ref[...] = proj(5)
    cm, sm = cm_ref[...], sm_ref[...]
    mq = _rotate_half(proj(6), cm, sm, ROT_DIM // 2)
    mk = _rotate_half(proj(7), cm, sm, ROT_DIM // 2)
    mv = proj(8)
    mk_ref[...] = mk.astype(BF16)
    nblk = mq.shape[0] // MOBA_BLOCK
    for j in range(nblk):
        rows = slice(j * MOBA_BLOCK, (j + 1) * MOBA_BLOCK)
        mqt_ref[j] = mq[rows].T
        mvt_ref[j] = mv[rows].T.astype(BF16)
        km_ref[0, j:j + 1, :] = jnp.mean(mk[rows], axis=0, keepdims=True)
    up_ref[...] = proj(9)


def _in_proj(x2, g, w_in, tabs, seq):
    T, D = x2.shape
    tm = ROW_TILE
    nt = T // tm
    per_seq = seq // tm
    nblk = tm // MOBA_BLOCK
    G = GROUP
    row = lambda i: (i, 0)
    tab = pl.BlockSpec((tm, G), lambda i: (i % per_seq, 0))
    f32o = jax.ShapeDtypeStruct((T, G), F32)
    bfo = jax.ShapeDtypeStruct((T, G), BF16)
    blk3 = lambda dt: jax.ShapeDtypeStruct((T // MOBA_BLOCK, MOBA_BLOCK, MOBA_BLOCK), dt)
    o_row = pl.BlockSpec((tm, G), row)
    o_blk = pl.BlockSpec((nblk, MOBA_BLOCK, MOBA_BLOCK), lambda i: (i, 0, 0))
    return pl.pallas_call(
        _in_proj_kernel,
        grid=(nt,),
        in_specs=[pl.BlockSpec((tm, D), row), _const_spec((1, D)), _const_spec(w_in.shape),
                  tab, tab, tab, tab],
        out_specs=[o_row, o_row, o_row, o_row, o_row, o_blk, o_row, o_blk,
                   pl.BlockSpec((1, nblk, G), lambda i: (i, 0, 0)), o_row],
        out_shape=[f32o, bfo, bfo, bfo, f32o, blk3(F32), bfo, blk3(BF16),
                   jax.ShapeDtypeStruct((nt, nblk, G), F32), f32o],
        compiler_params=_cparams(("arbitrary",)),
        name="in_proj",
    )(x2, g, w_in, *tabs)


def _local_kernel(hc_ref, up_ref, cw_ref, cb_ref, lg_ref, lb_ref, wp_ref, pb_ref, ps_ref,
                  ya_ref, yd_ref, hbuf, ubuf):
    ts = LOCAL_TILE
    s = pl.program_id(1)

    @pl.when(s == 0)
    def _():
        hbuf[0:CONV_HALO, :] = jnp.zeros((CONV_HALO, GROUP), F32)
        ubuf[0:POOL_HALO, :] = jnp.zeros((POOL_HALO, GROUP), F32)

    @pl.when(s > 0)
    def _():
        hbuf[0:CONV_HALO, :] = hbuf[ts:ts + CONV_HALO, :]
        ubuf[0:POOL_HALO, :] = ubuf[ts:ts + POOL_HALO, :]

    hbuf[CONV_HALO:, :] = hc_ref[...]
    ubuf[POOL_HALO:, :] = up_ref[...]

    rc = LOCAL_CHUNK
    lane_grp = lax.broadcasted_iota(jnp.int32, (rc, GROUP), 1) // (GROUP // len(POOL_WINDOWS))
    for c in range(ts // rc):
        r0 = c * rc
        base = CONV_HALO + r0 - (CONV_WIDTH - 1)
        acc = jnp.broadcast_to(cb_ref[...], (rc, GROUP))
        for j in range(CONV_WIDTH):
            acc = acc + cw_ref[j:j + 1, :] * hbuf[base + j:base + j + rc, :]
        mu = jnp.mean(acc, axis=-1, keepdims=True)
        d = acc - mu
        var = jnp.mean(d * d, axis=-1, keepdims=True)
        ya_ref[r0:r0 + rc, :] = _silu(d * lax.rsqrt(var + 1e-5) * lg_ref[...] + lb_ref[...]).astype(BF16)

        u0 = ubuf[POOL_HALO + r0:POOL_HALO + r0 + rc, :]
        t1 = (s * ts + r0 + 1 + lax.broadcasted_iota(jnp.int32, (rc, 1), 0)).astype(F32)
        run = u0
        pooled = jnp.zeros((rc, GROUP), F32)
        for dlt in range(1, max(POOL_WINDOWS)):
            run = run + ubuf[POOL_HALO + r0 - dlt:POOL_HALO + r0 - dlt + rc, :]
            if dlt + 1 in POOL_WINDOWS:
                gi = POOL_WINDOWS.index(dlt + 1)
                cnt = jnp.minimum(t1, float(dlt + 1))
                pooled = jnp.where(lane_grp == gi, run / cnt - u0, pooled)
        y = jnp.dot(pooled.astype(BF16), wp_ref[...], preferred_element_type=F32)
        yd_ref[r0:r0 + rc, :] = ((y + pb_ref[...]) * ps_ref[...]).astype(BF16)


def _local_mix(hc, up, cw, cb, lg, lb, wp, pb, ps, batch, seq):
    T, G = hc.shape
    ts = LOCAL_TILE
    ns = seq // ts
    row = pl.BlockSpec((ts, G), lambda b, s: (b * ns + s, 0))
    vec = _const_spec((1, G))
    return pl.pallas_call(
        _local_kernel,
        grid=(batch, ns),
        in_specs=[row, row, _const_spec(cw.shape), vec, vec, vec, _const_spec(wp.shape), vec, vec],
        out_specs=[row, row],
        out_shape=[jax.ShapeDtypeStruct((T, G), BF16)] * 2,
        scratch_shapes=[pltpu.VMEM((CONV_HALO + ts, G), F32), pltpu.VMEM((POOL_HALO + ts, G), F32)],
        compiler_params=_cparams(("arbitrary", "arbitrary")),
        name="local_mix",
    )(hc, up, cw, cb, lg, lb, wp, pb, ps)


def _split_dot(x, p):
    hi = x.astype(BF16)
    lo = (x - hi.astype(F32)).astype(BF16)
    return (jnp.dot(hi, p, preferred_element_type=F32) + jnp.dot(lo, p, preferred_element_type=F32))


def _retention_kernel(q_ref, k_ref, v_ref, g_ref, dm_ref, qd_ref, kd_ref, cd_ref, bd_ref, gg_ref, gb_ref,
                      y_ref, state):
    @pl.when(pl.program_id(1) == 0)
    def _():
        state[...] = jnp.zeros_like(state)

    q, k, v = q_ref[...], k_ref[...], v_ref[...]
    head = _lane_head((1, GROUP), 1)
    nt = (((1,), (1,)), ((), ()))
    o = jnp.dot(q, state[...].astype(BF16), preferred_element_type=F32) * qd_ref[...]
    for h in range(N_HEADS):
        mh = head == h
        qh = jnp.where(mh, q, jnp.zeros_like(q))
        sc = lax.dot_general(qh, k, nt, preferred_element_type=F32) * dm_ref[h]
        oh = jnp.dot(sc.astype(BF16), v, preferred_element_type=F32)
        o = o + jnp.where(mh, oh, 0.0)
    kdec = (k.astype(F32) * kd_ref[...]).astype(BF16)
    upd = lax.dot_general(kdec, v, (((0,), (0,)), ((), ())), preferred_element_type=F32)
    bd = bd_ref[...]
    state[...] = state[...] * cd_ref[...] + bd * upd

    p = (bd * (1.0 / HEAD_DIM)).astype(BF16)
    mu = _split_dot(o, p)
    d = o - mu
    var = _split_dot(d * d, p)
    on = d * lax.rsqrt(var + 1e-5) * gg_ref[...] + gb_ref[...]
    y_ref[...] = (_silu(g_ref[...]) * on).astype(BF16)


def _retention(rq, rk, rv, rg, rtabs, gg, gb, batch, seq):
    T, G = rq.shape
    C = RET_CHUNK
    nc = seq // C
    row = pl.BlockSpec((C, G), lambda b, c: (b * nc + c, 0))
    dm, qd, kd, cd, bd = rtabs
    vec = _const_spec((1, G))
    return pl.pallas_call(
        _retention_kernel,
        grid=(batch, nc),
        in_specs=[row, row, row, row, _const_spec(dm.shape), _const_spec(qd.shape), _const_spec(kd.shape),
                  vec, _const_spec(bd.shape), vec, vec],
        out_specs=row,
        out_shape=jax.ShapeDtypeStruct((T, G), BF16),
        scratch_shapes=[pltpu.VMEM((G, G), F32)],
        compiler_params=_cparams(("arbitrary", "arbitrary")),
        name="retention",
    )(rq, rk, rv, rg, dm, qd, kd, cd, bd, gg, gb)


def _moba_kernel(qt_ref, k_ref, vt_ref, km_ref, y_ref, sel_ref, ot_ref):
    n = pl.program_id(1)
    nb = km_ref.shape[1]
    Q = MOBA_BLOCK
    qt = qt_ref[0] * (HEAD_DIM ** -0.5)
    qtb = qt.astype(BF16)
    ch_head = _lane_head((GROUP, 1), 0)
    km = km_ref[0]
    blk = lax.broadcasted_iota(jnp.int32, (nb, Q), 0)
    krow = lax.broadcasted_iota(jnp.int32, (Q, Q), 0)
    qcol = lax.broadcasted_iota(jnp.int32, (Q, Q), 1)

    for h in range(N_HEADS):
        mh = ch_head == h
        qh32 = jnp.where(mh, qt, 0.0)
        qh = jnp.where(mh, qtb, jnp.zeros_like(qtb))
        hs = slice(h * HEAD_DIM, (h + 1) * HEAD_DIM)

        gate = jnp.dot(km, qh32, preferred_element_type=F32, precision=lax.Precision.HIGHEST)
        past = blk < n
        gate = jnp.where(past, gate, -jnp.inf)
        sel = jnp.zeros((nb, Q), F32)
        for _ in range(MOBA_TOPK):
            top = jnp.max(gate, axis=0, keepdims=True)
            first = jnp.min(jnp.where(gate == top, blk, nb), axis=0, keepdims=True)
            pick = (blk == first) & past
            sel = jnp.where(pick, 1.0, sel)
            gate = jnp.where(pick, -jnp.inf, gate)
        sel_ref[...] = sel

        s = jnp.dot(k_ref[n], qh, preferred_element_type=F32)
        s = jnp.where(krow <= qcol, s, NEG)
        m = jnp.max(s, axis=0, keepdims=True)
        p = jnp.exp(s - m)
        l = jnp.sum(p, axis=0, keepdims=True)
        acc = jnp.dot(vt_ref[n, hs, :], p.astype(BF16), preferred_element_type=F32)

        def body(j, carry):
            m, l, acc = carry
            on = sel_ref[pl.ds(j, 1), :] > 0.0
            s = jnp.dot(k_ref[j], qh, preferred_element_type=F32)
            m_new = jnp.where(on, jnp.maximum(m, jnp.max(s, axis=0, keepdims=True)), m)
            p = jnp.exp(s - jnp.where(on, m_new, BIG))
            alpha = jnp.exp(m - m_new)
            l = alpha * l + jnp.sum(p, axis=0, keepdims=True)
            pv = jnp.dot(vt_ref[j, hs, :], p.astype(BF16), preferred_element_type=F32)
            return m_new, l, alpha * acc + pv

        m, l, acc = lax.fori_loop(0, n, body, (m, l, acc))
        ot_ref[hs, :] = acc / l

    y_ref[...] = ot_ref[...].T.astype(BF16)


def _moba(mqt, mk3, mvt, km, batch, seq):
    nb = seq // MOBA_BLOCK
    B = MOBA_BLOCK
    G = GROUP
    return pl.pallas_call(
        _moba_kernel,
        grid=(batch, nb),
        in_specs=[pl.BlockSpec((1, G, B), lambda b, i: (b * nb + i, 0, 0)),
                  pl.BlockSpec((nb, B, G), lambda b, i: (b, 0, 0)),
                  pl.BlockSpec((nb, G, B), lambda b, i: (b, 0, 0)),
                  pl.BlockSpec((1, nb, G), lambda b, i: (b, 0, 0))],
        out_specs=pl.BlockSpec((B, G), lambda b, i: (b * nb + i, 0)),
        out_shape=jax.ShapeDtypeStruct((batch * seq, G), BF16),
        scratch_shapes=[pltpu.VMEM((nb, B), F32), pltpu.VMEM((G, B), F32)],
        compiler_params=_cparams(("arbitrary", "arbitrary")),
        name="moba",
    )(mqt, mk3, mvt, km)


def _post_kernel(x_ref, ya_ref, yb_ref, yc_ref, yd_ref, wo_ref, gpo_ref, gfi_ref,
                 wg_ref, wu_ref, wd_ref, gfo_ref, o_ref, f_scr):
    G = GROUP
    y = jnp.dot(ya_ref[...], wo_ref[0:G, :], preferred_element_type=F32)
    y = y + jnp.dot(yb_ref[...], wo_ref[G:2 * G, :], preferred_element_type=F32)
    y = y + jnp.dot(yc_ref[...], wo_ref[2 * G:3 * G, :], preferred_element_type=F32)
    y = y + jnp.dot(yd_ref[...], wo_ref[3 * G:4 * G, :], preferred_element_type=F32)
    x1 = x_ref[...] + _rms(y, gpo_ref[...])
    hb = _rms(x1, gfi_ref[...]).astype(BF16)
    d_ff = wg_ref.shape[1]
    for c in range(0, d_ff, FF_CHUNK):
        cols = slice(c, c + FF_CHUNK)
        gt = jnp.dot(hb, wg_ref[:, cols], preferred_element_type=F32)
        up = jnp.dot(hb, wu_ref[:, cols], preferred_element_type=F32)
        f_scr[:, cols] = (_silu(gt) * up).astype(BF16)
    f = jnp.dot(f_scr[...], wd_ref[...], preferred_element_type=F32)
    o_ref[...] = x1 + _rms(f, gfo_ref[...])


def _post(x2, ya, yb, yc, yd, wo, gpo, gfi, wg, wu, wd, gfo):
    T, D = x2.shape
    tm = ROW_TILE
    G = GROUP
    d_ff = wg.shape[1]
    row = lambda i: (i, 0)
    xs = pl.BlockSpec((tm, D), row)
    ys = pl.BlockSpec((tm, G), row)
    vec = _const_spec((1, D))
    return pl.pallas_call(
        _post_kernel,
        grid=(T // tm,),
        in_specs=[xs, ys, ys, ys, ys, _const_spec(wo.shape), vec, vec,
                  _const_spec(wg.shape), _const_spec(wu.shape), _const_spec(wd.shape), vec],
        out_specs=xs,
        out_shape=jax.ShapeDtypeStruct((T, D), F32),
        scratch_shapes=[pltpu.VMEM((tm, d_ff), BF16)],
        compiler_params=_cparams(("arbitrary",)),
        name="out_proj_ffn",
    )(x2, ya, yb, yc, yd, wo, gpo, gfi, wg, wu, wd, gfo)


def _rotary_tables(seq, rot_dim, theta):
    pos = jnp.arange(seq, dtype=F32)
    inv = 1.0 / (theta ** (jnp.arange(0, rot_dim, 2, dtype=F32) / rot_dim))
    ang = pos[:, None] * inv[None, :]
    cos, sin = jnp.cos(ang), jnp.sin(ang)
    rest = HEAD_DIM - rot_dim
    cos_h = jnp.concatenate([cos, cos, jnp.ones((seq, rest), F32)], axis=-1)
    sin_h = jnp.concatenate([-sin, sin, jnp.zeros((seq, rest), F32)], axis=-1)
    return jnp.tile(cos_h, (1, N_HEADS)), jnp.tile(sin_h, (1, N_HEADS))


def _retention_tables():
    c = RET_CHUNK
    log_gamma = jnp.log(1.0 - 2.0 ** (-5.0 - jnp.arange(N_HEADS, dtype=F32)))
    idx = jnp.arange(c, dtype=F32)
    diff = idx[:, None] - idx[None, :]
    dm = jnp.where(diff[None] >= 0, jnp.exp(jnp.maximum(diff, 0.0)[None] * log_gamma[:, None, None]), 0.0)
    lanes = lambda t: jnp.repeat(t, HEAD_DIM, axis=-1)
    qd = lanes(jnp.exp((idx + 1.0)[:, None] * log_gamma[None, :]))
    kd = lanes(jnp.exp((c - 1.0 - idx)[:, None] * log_gamma[None, :]))
    cd = lanes(jnp.exp(c * log_gamma)[None, :])
    head = jnp.arange(GROUP) // HEAD_DIM
    bd = (head[:, None] == head[None, :]).astype(F32)
    return dm, qd, kd, cd, bd


def _block_diag(w):
    n, c, _ = w.shape
    eye = jnp.eye(n, dtype=w.dtype)
    return (eye[:, None, :, None] * w[:, :, None, :]).reshape(n * c, n * c)


def kernel(x, attn_pre_g, w_in, conv_w, conv_b, conv_ln_g, conv_ln_b, ret_gn_g, ret_gn_b, pool_w, pool_b,
           pool_scale, w_out, attn_post_g, ffn_pre_g, w_gate, w_up, w_down, ffn_post_g):
    batch, seq, d_model = x.shape
    depth = w_in.shape[0]
    assert seq % ROW_TILE == 0 and seq % LOCAL_TILE == 0 and seq % RET_CHUNK == 0 and seq % MOBA_BLOCK == 0
    nb = seq // MOBA_BLOCK
    tabs = _rotary_tables(seq, HEAD_DIM, RET_ROT_THETA) + _rotary_tables(seq, ROT_DIM, ROPE_THETA)
    rtabs = _retention_tables()
    vec = lambda a: a.reshape(1, -1)

    x2 = x.reshape(batch * seq, d_model)
    for l in range(depth):
        hc, rq, rk, rv, rg, mqt, mk, mvt, km, up = _in_proj(
            x2, vec(attn_pre_g[l]), w_in[l].astype(BF16), tabs, seq)
        ya, yd = _local_mix(hc, up, conv_w[l], vec(conv_b[l]), vec(conv_ln_g[l]), vec(conv_ln_b[l]),
                            _block_diag(pool_w[l]).astype(BF16), vec(pool_b[l]), vec(pool_scale[l]), batch, seq)
        yb = _retention(rq, rk, rv, rg, rtabs, vec(ret_gn_g[l]), vec(ret_gn_b[l]), batch, seq)
        yc = _moba(mqt, mk.reshape(batch * nb, MOBA_BLOCK, GROUP), mvt, km.reshape(batch, nb, GROUP), batch, seq)
        x2 = _post(x2, ya, yb, yc, yd, w_out[l].astype(BF16), vec(attn_post_g[l]), vec(ffn_pre_g[l]),
                   w_gate[l].astype(BF16), w_up[l].astype(BF16), w_down[l].astype(BF16), vec(ffn_post_g[l]))
    return x2.reshape(batch, seq, d_model)
```

```python
import functools
import math

import jax
import jax.numpy as jnp
from jax import lax
from jax.experimental import pallas as pl
from jax.experimental.pallas import tpu as pltpu

F32 = jnp.float32
BF16 = jnp.bfloat16

GROUP = 256
HEAD_DIM = 64
N_HEADS = GROUP // HEAD_DIM
CONV_WIDTH = 31
RET_ROT_THETA = 10000.0
MOBA_BLOCK = 256
MOBA_TOPK = 3
ROPE_THETA = 500000.0
ROT_DIM = HEAD_DIM // 4
POOL_WINDOWS = (2, 4, 8, 16)
NEG = -1e30
BIG = 1e30
VT_ROWS = HEAD_DIM + 16
LOG2E = math.log2(math.e)

ROW_TILE = 512
LOCAL_TILE = 512
LOCAL_CHUNK = 64
CONV_HALO = 32
POOL_HALO = 16
RET_CHUNK = 256
FF_CHUNK = 256
VMEM_LIMIT = 56 * 1024 * 1024


def _cparams(sem):
    return pltpu.CompilerParams(dimension_semantics=sem, vmem_limit_bytes=VMEM_LIMIT)


def _const_spec(shape):
    nd = len(shape)
    return pl.BlockSpec(shape, lambda *_: (0,) * nd, pipeline_mode=pl.Buffered(1))


def _rms(x, g):
    return x * lax.rsqrt(jnp.mean(x * x, axis=-1, keepdims=True) + 1e-6) * g


def _silu(x):
    return x * jax.nn.sigmoid(x)


def _lane_head(shape, axis):
    return lax.broadcasted_iota(jnp.int32, shape, axis) // HEAD_DIM


def _rotate_half(x, cos, sin_signed, half):
    n = x.shape[-1]
    lane = lax.broadcasted_iota(jnp.int32, x.shape, 1) % HEAD_DIM
    partner = jnp.where(lane < half, pltpu.roll(x, n - half, 1), pltpu.roll(x, half, 1))
    return x * cos + partner * sin_signed


def _in_proj_kernel(x_ref, g_ref, w_ref, cr_ref, sr_ref, cm_ref, sm_ref,
                    hc_ref, rq_ref, rk_ref, rv_ref, rg_ref, mqt_ref, mk_ref, mvt_ref, km_ref, up_ref):
    G = GROUP
    hb = _rms(x_ref[...], g_ref[...]).astype(BF16)

    def proj(i, n=1):
        return jnp.dot(hb, w_ref[:, i * G:(i + n) * G], preferred_element_type=F32)

    uc = proj(0, 2)
    hc_ref[...] = uc[:, :G] * jax.nn.sigmoid(uc[:, G:])
    cr, sr = cr_ref[...], sr_ref[...]
    rq_ref[...] = _rotate_half(proj(2), cr, sr, HEAD_DIM // 2).astype(BF16)
    rk_ref[...] = (_rotate_half(proj(3), cr, sr, HEAD_DIM // 2) * (HEAD_DIM ** -0.5)).astype(BF16)
    rv_ref[...] = proj(4).astype(BF16)
    rg_ref[...] = proj(5)
    cm, sm = cm_ref[...], sm_ref[...]
    mq = _rotate_half(proj(6), cm, sm, ROT_DIM // 2)
    mk = _rotate_half(proj(7), cm, sm, ROT_DIM // 2)
    mv = proj(8)
    mk_ref[...] = mk.astype(BF16)
    nblk = mq.shape[0] // MOBA_BLOCK
    for j in range(nblk):
        rows = slice(j * MOBA_BLOCK, (j + 1) * MOBA_BLOCK)
        mqt_ref[j] = mq[rows].T
        vt = mv[rows].T.astype(BF16)
        for h in range(N_HEADS):
            r = h * VT_ROWS
            mvt_ref[j, r:r + HEAD_DIM, :] = vt[h * HEAD_DIM:(h + 1) * HEAD_DIM]
            mvt_ref[j, r + HEAD_DIM:r + VT_ROWS, :] = jnp.ones((VT_ROWS - HEAD_DIM, MOBA_BLOCK), BF16)
        km_ref[0, j:j + 1, :] = jnp.mean(mk[rows], axis=0, keepdims=True)
    up_ref[...] = proj(9)


def _in_proj(x2, g, w_in, tabs, seq):
    T, D = x2.shape
    tm = ROW_TILE
    nt = T // tm
    per_seq = seq // tm
    nblk = tm // MOBA_BLOCK
    G = GROUP
    row = lambda i: (i, 0)
    tab = pl.BlockSpec((tm, G), lambda i: (i % per_seq, 0))
    f32o = jax.ShapeDtypeStruct((T, G), F32)
    bfo = jax.ShapeDtypeStruct((T, G), BF16)
    blk3 = lambda rows, dt: jax.ShapeDtypeStruct((T // MOBA_BLOCK, rows, MOBA_BLOCK), dt)
    o_row = pl.BlockSpec((tm, G), row)
    o_blk = lambda rows: pl.BlockSpec((nblk, rows, MOBA_BLOCK), lambda i: (i, 0, 0))
    return pl.pallas_call(
        _in_proj_kernel,
        grid=(nt,),
        in_specs=[pl.BlockSpec((tm, D), row), _const_spec((1, D)), _const_spec(w_in.shape),
                  tab, tab, tab, tab],
        out_specs=[o_row, o_row, o_row, o_row, o_row, o_blk(G), o_row, o_blk(N_HEADS * VT_ROWS),
                   pl.BlockSpec((1, nblk, G), lambda i: (i, 0, 0)), o_row],
        out_shape=[f32o, bfo, bfo, bfo, f32o, blk3(G, F32), bfo, blk3(N_HEADS * VT_ROWS, BF16),
                   jax.ShapeDtypeStruct((nt, nblk, G), F32), f32o],
        compiler_params=_cparams(("arbitrary",)),
        name="in_proj",
    )(x2, g, w_in, *tabs)


def _local_kernel(hc_ref, up_ref, cw_ref, cb_ref, lg_ref, lb_ref, wp_ref, pb_ref, ps_ref,
                  ya_ref, yd_ref, hbuf, ubuf):
    ts = LOCAL_TILE
    s = pl.program_id(1)

    @pl.when(s == 0)
    def _():
        hbuf[0:CONV_HALO, :] = jnp.zeros((CONV_HALO, GROUP), F32)
        ubuf[0:POOL_HALO, :] = jnp.zeros((POOL_HALO, GROUP), F32)

    @pl.when(s > 0)
    def _():
        hbuf[0:CONV_HALO, :] = hbuf[ts:ts + CONV_HALO, :]
        ubuf[0:POOL_HALO, :] = ubuf[ts:ts + POOL_HALO, :]

    hbuf[CONV_HALO:, :] = hc_ref[...]
    ubuf[POOL_HALO:, :] = up_ref[...]

    rc = LOCAL_CHUNK
    lane_grp = lax.broadcasted_iota(jnp.int32, (rc, GROUP), 1) // (GROUP // len(POOL_WINDOWS))
    for c in range(ts // rc):
        r0 = c * rc
        base = CONV_HALO + r0 - (CONV_WIDTH - 1)
        acc = jnp.broadcast_to(cb_ref[...], (rc, GROUP))
        for j in range(CONV_WIDTH):
            acc = acc + cw_ref[j:j + 1, :] * hbuf[base + j:base + j + rc, :]
        mu = jnp.mean(acc, axis=-1, keepdims=True)
        d = acc - mu
        var = jnp.mean(d * d, axis=-1, keepdims=True)
        ya_ref[r0:r0 + rc, :] = _silu(d * lax.rsqrt(var + 1e-5) * lg_ref[...] + lb_ref[...]).astype(BF16)

        u0 = ubuf[POOL_HALO + r0:POOL_HALO + r0 + rc, :]
        t1 = (s * ts + r0 + 1 + lax.broadcasted_iota(jnp.int32, (rc, 1), 0)).astype(F32)
        run = u0
        pooled = jnp.zeros((rc, GROUP), F32)
        for dlt in range(1, max(POOL_WINDOWS)):
            run = run + ubuf[POOL_HALO + r0 - dlt:POOL_HALO + r0 - dlt + rc, :]
            if dlt + 1 in POOL_WINDOWS:
                gi = POOL_WINDOWS.index(dlt + 1)
                cnt = jnp.minimum(t1, float(dlt + 1))
                pooled = jnp.where(lane_grp == gi, run / cnt - u0, pooled)
        y = jnp.dot(pooled.astype(BF16), wp_ref[...], preferred_element_type=F32)
        yd_ref[r0:r0 + rc, :] = ((y + pb_ref[...]) * ps_ref[...]).astype(BF16)


def _local_mix(hc, up, cw, cb, lg, lb, wp, pb, ps, batch, seq):
    T, G = hc.shape
    ts = LOCAL_TILE
    ns = seq // ts
    row = pl.BlockSpec((ts, G), lambda b, s: (b * ns + s, 0))
    vec = _const_spec((1, G))
    return pl.pallas_call(
        _local_kernel,
        grid=(batch, ns),
        in_specs=[row, row, _const_spec(cw.shape), vec, vec, vec, _const_spec(wp.shape), vec, vec],
        out_specs=[row, row],
        out_shape=[jax.ShapeDtypeStruct((T, G), BF16)] * 2,
        scratch_shapes=[pltpu.VMEM((CONV_HALO + ts, G), F32), pltpu.VMEM((POOL_HALO + ts, G), F32)],
        compiler_params=_cparams(("arbitrary", "arbitrary")),
        name="local_mix",
    )(hc, up, cw, cb, lg, lb, wp, pb, ps)


def _split_dot(x, p):
    hi = x.astype(BF16)
    lo = (x - hi.astype(F32)).astype(BF16)
    return (jnp.dot(hi, p, preferred_element_type=F32) + jnp.dot(lo, p, preferred_element_type=F32))


def _retention_kernel(q_ref, k_ref, v_ref, g_ref, dm_ref, qd_ref, kd_ref, cd_ref, bd_ref, gg_ref, gb_ref,
                      y_ref, state):
    @pl.when(pl.program_id(1) == 0)
    def _():
        state[...] = jnp.zeros_like(state)

    q, k, v = q_ref[...], k_ref[...], v_ref[...]
    head = _lane_head((1, GROUP), 1)
    nt = (((1,), (1,)), ((), ()))
    o = jnp.dot(q, state[...].astype(BF16), preferred_element_type=F32) * qd_ref[...]
    for h in range(N_HEADS):
        mh = head == h
        qh = jnp.where(mh, q, jnp.zeros_like(q))
        sc = lax.dot_general(qh, k, nt, preferred_element_type=F32) * dm_ref[h]
        oh = jnp.dot(sc.astype(BF16), v, preferred_element_type=F32)
        o = o + jnp.where(mh, oh, 0.0)
    kdec = (k.astype(F32) * kd_ref[...]).astype(BF16)
    upd = lax.dot_general(kdec, v, (((0,), (0,)), ((), ())), preferred_element_type=F32)
    bd = bd_ref[...]
    state[...] = state[...] * cd_ref[...] + bd * upd

    p = (bd * (1.0 / HEAD_DIM)).astype(BF16)
    mu = _split_dot(o, p)
    d = o - mu
    var = _split_dot(d * d, p)
    on = d * lax.rsqrt(var + 1e-5) * gg_ref[...] + gb_ref[...]
    y_ref[...] = (_silu(g_ref[...]) * on).astype(BF16)


def _retention(rq, rk, rv, rg, rtabs, gg, gb, batch, seq):
    T, G = rq.shape
    C = RET_CHUNK
    nc = seq // C
    row = pl.BlockSpec((C, G), lambda b, c: (b * nc + c, 0))
    dm, qd, kd, cd, bd = rtabs
    vec = _const_spec((1, G))
    return pl.pallas_call(
        _retention_kernel,
        grid=(batch, nc),
        in_specs=[row, row, row, row, _const_spec(dm.shape), _const_spec(qd.shape), _const_spec(kd.shape),
                  vec, _const_spec(bd.shape), vec, vec],
        out_specs=row,
        out_shape=jax.ShapeDtypeStruct((T, G), BF16),
        scratch_shapes=[pltpu.VMEM((G, G), F32)],
        compiler_params=_cparams(("arbitrary", "arbitrary")),
        name="retention",
    )(rq, rk, rv, rg, dm, qd, kd, cd, bd, gg, gb)


def _moba_kernel(qt_ref, k_ref, vt_ref, km_ref, y_ref, sel_ref, qh_ref, m_ref, acc_ref, ot_ref):
    n = pl.program_id(1)
    nb = km_ref.shape[1]
    Q = MOBA_BLOCK
    qt = qt_ref[0]
    qs = (qt * (HEAD_DIM ** -0.5 * LOG2E)).astype(BF16)
    ch_head = _lane_head((GROUP, 1), 0)
    km = km_ref[0]
    km_head = _lane_head((1, GROUP), 1)
    blk = lax.broadcasted_iota(jnp.int32, (nb, Q), 0)
    past = blk < n

    km_heads = jnp.concatenate([jnp.where(km_head == h, km, 0.0) for h in range(N_HEADS)], axis=0)
    gate_all = jnp.dot(km_heads, qt, preferred_element_type=F32, precision=lax.Precision.HIGHEST)
    for h in range(N_HEADS):
        gate = jnp.where(past, gate_all[h * nb:(h + 1) * nb], -jnp.inf)
        sel = jnp.zeros((nb, Q), F32)
        for _ in range(MOBA_TOPK):
            top = jnp.max(gate, axis=0, keepdims=True)
            first = jnp.min(jnp.where(gate == top, blk, nb), axis=0, keepdims=True)
            pick = (blk == first) & past
            sel = jnp.where(pick, 1.0, sel)
            gate = jnp.where(pick, -jnp.inf, gate)
        sel_ref[h * nb:(h + 1) * nb, :] = sel
        qh_ref[h] = jnp.where(ch_head == h, qs, jnp.zeros_like(qs))

    m_ref[...] = jnp.full(m_ref.shape, -BIG, F32)
    acc_ref[...] = jnp.zeros(acc_ref.shape, F32)

    def scores_of(j):
        kb = k_ref[j]
        return tuple(jnp.dot(kb, qh_ref[h], preferred_element_type=F32) for h in range(N_HEADS))

    def absorb(j, scores, own):
        for h in range(N_HEADS):
            s = scores[h]
            vt = vt_ref[j, h * VT_ROWS:(h + 1) * VT_ROWS, :]
            m = m_ref[h:h + 1, :]
            if own:
                krow = lax.broadcasted_iota(jnp.int32, (Q, Q), 0)
                qcol = lax.broadcasted_iota(jnp.int32, (Q, Q), 1)
                s = jnp.where(krow <= qcol, s, NEG)
                m_new = jnp.maximum(m, jnp.max(s, axis=0, keepdims=True))
                shift = m_new
            else:
                on = sel_ref[pl.ds(h * nb + j, 1), :] > 0.0
                m_new = jnp.where(on, jnp.maximum(m, jnp.max(s, axis=0, keepdims=True)), m)
                shift = jnp.where(on, m_new, BIG)
            p = jnp.exp2(s - shift).astype(BF16)
            acc_ref[h] = jnp.exp2(m - m_new) * acc_ref[h] + jnp.dot(vt, p, preferred_element_type=F32)
            m_ref[h:h + 1, :] = m_new

    def body(j, scores):
        nxt = scores_of(j + 1)
        absorb(j, scores, False)
        return nxt

    absorb(n, lax.fori_loop(0, n, body, scores_of(0)), True)
    for h in range(N_HEADS):
        a = acc_ref[h]
        ot_ref[h * HEAD_DIM:(h + 1) * HEAD_DIM, :] = a[:HEAD_DIM] / a[HEAD_DIM:HEAD_DIM + 1]
    y_ref[...] = ot_ref[...].T.astype(BF16)


def _moba(mqt, mk3, mvt, km, batch, seq):
    nb = seq // MOBA_BLOCK
    B = MOBA_BLOCK
    G = GROUP
    return pl.pallas_call(
        _moba_kernel,
        grid=(batch, nb),
        in_specs=[pl.BlockSpec((1, G, B), lambda b, i: (b * nb + i, 0, 0)),
                  pl.BlockSpec((nb, B, G), lambda b, i: (b, 0, 0)),
                  pl.BlockSpec((nb, N_HEADS * VT_ROWS, B), lambda b, i: (b, 0, 0)),
                  pl.BlockSpec((1, nb, G), lambda b, i: (b, 0, 0))],
        out_specs=pl.BlockSpec((B, G), lambda b, i: (b * nb + i, 0)),
        out_shape=jax.ShapeDtypeStruct((batch * seq, G), BF16),
        scratch_shapes=[pltpu.VMEM((N_HEADS * nb, B), F32),
                        pltpu.VMEM((N_HEADS, G, B), BF16),
                        pltpu.VMEM((8, B), F32),
                        pltpu.VMEM((N_HEADS, VT_ROWS, B), F32),
                        pltpu.VMEM((G, B), F32)],
        compiler_params=_cparams(("arbitrary", "arbitrary")),
        name="moba",
    )(mqt, mk3, mvt, km)


def _post_kernel(x_ref, ya_ref, yb_ref, yc_ref, yd_ref, wo_ref, gpo_ref, gfi_ref,
                 wg_ref, wu_ref, wd_ref, gfo_ref, o_ref, f_scr):
    G = GROUP
    y = jnp.dot(ya_ref[...], wo_ref[0:G, :], preferred_element_type=F32)
    y = y + jnp.dot(yb_ref[...], wo_ref[G:2 * G, :], preferred_element_type=F32)
    y = y + jnp.dot(yc_ref[...], wo_ref[2 * G:3 * G, :], preferred_element_type=F32)
    y = y + jnp.dot(yd_ref[...], wo_ref[3 * G:4 * G, :], preferred_element_type=F32)
    x1 = x_ref[...] + _rms(y, gpo_ref[...])
    hb = _rms(x1, gfi_ref[...]).astype(BF16)
    d_ff = wg_ref.shape[1]
    for c in range(0, d_ff, FF_CHUNK):
        cols = slice(c, c + FF_CHUNK)
        gt = jnp.dot(hb, wg_ref[:, cols], preferred_element_type=F32)
        up = jnp.dot(hb, wu_ref[:, cols], preferred_element_type=F32)
        f_scr[:, cols] = (_silu(gt) * up).astype(BF16)
    f = jnp.dot(f_scr[...], wd_ref[...], preferred_element_type=F32)
    o_ref[...] = x1 + _rms(f, gfo_ref[...])


def _post(x2, ya, yb, yc, yd, wo, gpo, gfi, wg, wu, wd, gfo):
    T, D = x2.shape
    tm = ROW_TILE
    G = GROUP
    d_ff = wg.shape[1]
    row = lambda i: (i, 0)
    xs = pl.BlockSpec((tm, D), row)
    ys = pl.BlockSpec((tm, G), row)
    vec = _const_spec((1, D))
    return pl.pallas_call(
        _post_kernel,
        grid=(T // tm,),
        in_specs=[xs, ys, ys, ys, ys, _const_spec(wo.shape), vec, vec,
                  _const_spec(wg.shape), _const_spec(wu.shape), _const_spec(wd.shape), vec],
        out_specs=xs,
        out_shape=jax.ShapeDtypeStruct((T, D), F32),
        scratch_shapes=[pltpu.VMEM((tm, d_ff), BF16)],
        compiler_params=_cparams(("arbitrary",)),
        name="out_proj_ffn",
    )(x2, ya, yb, yc, yd, wo, gpo, gfi, wg, wu, wd, gfo)


def _rotary_tables(seq, rot_dim, theta):
    pos = jnp.arange(seq, dtype=F32)
    inv = 1.0 / (theta ** (jnp.arange(0, rot_dim, 2, dtype=F32) / rot_dim))
    ang = pos[:, None] * inv[None, :]
    cos, sin = jnp.cos(ang), jnp.sin(ang)
    rest = HEAD_DIM - rot_dim
    cos_h = jnp.concatenate([cos, cos, jnp.ones((seq, rest), F32)], axis=-1)
    sin_h = jnp.concatenate([-sin, sin, jnp.zeros((seq, rest), F32)], axis=-1)
    return jnp.tile(cos_h, (1, N_HEADS)), jnp.tile(sin_h, (1, N_HEADS))


def _retention_tables():
    c = RET_CHUNK
    log_gamma = jnp.log(1.0 - 2.0 ** (-5.0 - jnp.arange(N_HEADS, dtype=F32)))
    idx = jnp.arange(c, dtype=F32)
    diff = idx[:, None] - idx[None, :]
    dm = jnp.where(diff[None] >= 0, jnp.exp(jnp.maximum(diff, 0.0)[None] * log_gamma[:, None, None]), 0.0)
    lanes = lambda t: jnp.repeat(t, HEAD_DIM, axis=-1)
    qd = lanes(jnp.exp((idx + 1.0)[:, None] * log_gamma[None, :]))
    kd = lanes(jnp.exp((c - 1.0 - idx)[:, None] * log_gamma[None, :]))
    cd = lanes(jnp.exp(c * log_gamma)[None, :])
    head = jnp.arange(GROUP) // HEAD_DIM
    bd = (head[:, None] == head[None, :]).astype(F32)
    return dm, qd, kd, cd, bd


def _block_diag(w):
    n, c, _ = w.shape
    eye = jnp.eye(n, dtype=w.dtype)
    return (eye[:, None, :, None] * w[:, :, None, :]).reshape(n * c, n * c)


def kernel(x, attn_pre_g, w_in, conv_w, conv_b, conv_ln_g, conv_ln_b, ret_gn_g, ret_gn_b, pool_w, pool_b,
           pool_scale, w_out, attn_post_g, ffn_pre_g, w_gate, w_up, w_down, ffn_post_g):
    batch, seq, d_model = x.shape
    depth = w_in.shape[0]
    assert seq % ROW_TILE == 0 and seq % LOCAL_TILE == 0 and seq % RET_CHUNK == 0 and seq % MOBA_BLOCK == 0
    nb = seq // MOBA_BLOCK
    tabs = _rotary_tables(seq, HEAD_DIM, RET_ROT_THETA) + _rotary_tables(seq, ROT_DIM, ROPE_THETA)
    rtabs = _retention_tables()
    vec = lambda a: a.reshape(1, -1)

    x2 = x.reshape(batch * seq, d_model)
    for l in range(depth):
        hc, rq, rk, rv, rg, mqt, mk, mvt, km, up = _in_proj(
            x2, vec(attn_pre_g[l]), w_in[l].astype(BF16), tabs, seq)
        ya, yd = _local_mix(hc, up, conv_w[l], vec(conv_b[l]), vec(conv_ln_g[l]), vec(conv_ln_b[l]),
                            _block_diag(pool_w[l]).astype(BF16), vec(pool_b[l]), vec(pool_scale[l]), batch, seq)
        yb = _retention(rq, rk, rv, rg, rtabs, vec(ret_gn_g[l]), vec(ret_gn_b[l]), batch, seq)
        yc = _moba(mqt, mk.reshape(batch * nb, MOBA_BLOCK, GROUP), mvt, km.reshape(batch, nb, GROUP), batch, seq)
        x2 = _post(x2, ya, yb, yc, yd, w_out[l].astype(BF16), vec(attn_post_g[l]), vec(ffn_pre_g[l]),
                   w_gate[l].astype(BF16), w_up[l].astype(BF16), w_down[l].astype(BF16), vec(ffn_post_g[l]))
    return x2.reshape(batch, seq, d_model)
```

```python
import functools
import math

import jax
import jax.numpy as jnp
from jax import lax
from jax.experimental import pallas as pl
from jax.experimental.pallas import tpu as pltpu

F32 = jnp.float32
BF16 = jnp.bfloat16

GROUP = 256
HEAD_DIM = 64
N_HEADS = GROUP // HEAD_DIM
CONV_WIDTH = 31
RET_ROT_THETA = 10000.0
MOBA_BLOCK = 256
MOBA_TOPK = 3
ROPE_THETA = 500000.0
ROT_DIM = HEAD_DIM // 4
POOL_WINDOWS = (2, 4, 8, 16)
NEG = -1e30
BIG = 1e30
VT_ROWS = HEAD_DIM + 16
LOG2E = math.log2(math.e)

ROW_TILE = 512
LOCAL_TILE = 512
LOCAL_CHUNK = 64
CONV_HALO = 32
POOL_HALO = 16
RET_CHUNK = 256
FF_CHUNK = 256
VMEM_LIMIT = 56 * 1024 * 1024


def _cparams(sem):
    return pltpu.CompilerParams(dimension_semantics=sem, vmem_limit_bytes=VMEM_LIMIT)


def _const_spec(shape):
    nd = len(shape)
    return pl.BlockSpec(shape, lambda *_: (0,) * nd, pipeline_mode=pl.Buffered(1))


def _rms(x, g):
    return x * lax.rsqrt(jnp.mean(x * x, axis=-1, keepdims=True) + 1e-6) * g


def _silu(x):
    return x * jax.nn.sigmoid(x)


def _lane_head(shape, axis):
    return lax.broadcasted_iota(jnp.int32, shape, axis) // HEAD_DIM


def _rotate_half(x, cos, sin_signed, half):
    n = x.shape[-1]
    lane = lax.broadcasted_iota(jnp.int32, x.shape, 1) % HEAD_DIM
    partner = jnp.where(lane < half, pltpu.roll(x, n - half, 1), pltpu.roll(x, half, 1))
    return x * cos + partner * sin_signed


def _in_proj_kernel(x_ref, g_ref, w_ref, cr_ref, sr_ref, cm_ref, sm_ref,
                    hc_ref, rq_ref, rk_ref, rv_ref, rg_ref, mqt_ref, mk_ref, mvt_ref, km_ref, up_ref):
    G = GROUP
    hb = _rms(x_ref[...], g_ref[...]).astype(BF16)

    def proj(i, n=1):
        return jnp.dot(hb, w_ref[:, i * G:(i + n) * G], preferred_element_type=F32)

    uc = proj(0, 2)
    hc_ref[...] = uc[:, :G] * jax.nn.sigmoid(uc[:, G:])
    cr, sr = cr_ref[...], sr_ref[...]
    rq_ref[...] = _rotate_half(proj(2), cr, sr, HEAD_DIM // 2).astype(BF16)
    rk_ref[...] = (_rotate_half(proj(3), cr, sr, HEAD_DIM // 2) * (HEAD_DIM ** -0.5)).astype(BF16)
    rv_ref[...] = proj(4).astype(BF16)
    rg_ref[...] = proj(5)
    cm, sm = cm_ref[...], sm_ref[...]
    mq = _rotate_half(proj(6), cm, sm, ROT_DIM // 2)
    mk = _rotate_half(proj(7), cm, sm, ROT_DIM // 2)
    mv = proj(8)
    mk_ref[...] = mk.astype(BF16)
    nblk = mq.shape[0] // MOBA_BLOCK
    for j in range(nblk):
        rows = slice(j * MOBA_BLOCK, (j + 1) * MOBA_BLOCK)
        mqt_ref[j] = mq[rows].T
        vt = mv[rows].T.astype(BF16)
        for h in range(N_HEADS):
            r = h * VT_ROWS
            mvt_ref[j, r:r + HEAD_DIM, :] = vt[h * HEAD_DIM:(h + 1) * HEAD_DIM]
            mvt_ref[j, r + HEAD_DIM:r + VT_ROWS, :] = jnp.ones((VT_ROWS - HEAD_DIM, MOBA_BLOCK), BF16)
        km_ref[0, j:j + 1, :] = jnp.mean(mk[rows], axis=0, keepdims=True)
    up_ref[...] = proj(9)


def _in_proj(x2, g, w_in, tabs, seq):
    T, D = x2.shape
    tm = ROW_TILE
    nt = T // tm
    per_seq = seq // tm
    nblk = tm // MOBA_BLOCK
    G = GROUP
    row = lambda i: (i, 0)
    tab = pl.BlockSpec((tm, G), lambda i: (i % per_seq, 0))
    f32o = jax.ShapeDtypeStruct((T, G), F32)
    bfo = jax.ShapeDtypeStruct((T, G), BF16)
    blk3 = lambda rows, dt: jax.ShapeDtypeStruct((T // MOBA_BLOCK, rows, MOBA_BLOCK), dt)
    o_row = pl.BlockSpec((tm, G), row)
    o_blk = lambda rows: pl.BlockSpec((nblk, rows, MOBA_BLOCK), lambda i: (i, 0, 0))
    return pl.pallas_call(
        _in_proj_kernel,
        grid=(nt,),
        in_specs=[pl.BlockSpec((tm, D), row), _const_spec((1, D)), _const_spec(w_in.shape),
                  tab, tab, tab, tab],
        out_specs=[o_row, o_row, o_row, o_row, o_row, o_blk(G), o_row, o_blk(N_HEADS * VT_ROWS),
                   pl.BlockSpec((1, nblk, G), lambda i: (i, 0, 0)), o_row],
        out_shape=[f32o, bfo, bfo, bfo, f32o, blk3(G, F32), bfo, blk3(N_HEADS * VT_ROWS, BF16),
                   jax.ShapeDtypeStruct((nt, nblk, G), F32), f32o],
        compiler_params=_cparams(("arbitrary",)),
        name="in_proj",
    )(x2, g, w_in, *tabs)


def _local_kernel(hc_ref, up_ref, cw_ref, cb_ref, lg_ref, lb_ref, wp_ref, pb_ref, ps_ref,
                  ya_ref, yd_ref, hbuf, ubuf):
    ts = LOCAL_TILE
    s = pl.program_id(1)

    @pl.when(s == 0)
    def _():
        hbuf[0:CONV_HALO, :] = jnp.zeros((CONV_HALO, GROUP), F32)
        ubuf[0:POOL_HALO, :] = jnp.zeros((POOL_HALO, GROUP), F32)

    @pl.when(s > 0)
    def _():
        hbuf[0:CONV_HALO, :] = hbuf[ts:ts + CONV_HALO, :]
        ubuf[0:POOL_HALO, :] = ubuf[ts:ts + POOL_HALO, :]

    hbuf[CONV_HALO:, :] = hc_ref[...]
    ubuf[POOL_HALO:, :] = up_ref[...]

    rc = LOCAL_CHUNK
    lane_grp = lax.broadcasted_iota(jnp.int32, (rc, GROUP), 1) // (GROUP // len(POOL_WINDOWS))
    for c in range(ts // rc):
        r0 = c * rc
        back = 8 * ((CONV_WIDTH - 1) // 8)
        acc = jnp.broadcast_to(cb_ref[...], (rc, GROUP))
        for r in range(8):
            lo = CONV_HALO + r0 - back - r
            hr = hbuf[lo:lo + back + rc, :]
            part = None
            for q in range(back // 8 + 1):
                d = 8 * q + r
                if d < CONV_WIDTH:
                    j = CONV_WIDTH - 1 - d
                    term = cw_ref[j:j + 1, :] * hr[back - 8 * q:back - 8 * q + rc]
                    part = term if part is None else part + term
            acc = acc + part
        mu = jnp.mean(acc, axis=-1, keepdims=True)
        d = acc - mu
        var = jnp.mean(d * d, axis=-1, keepdims=True)
        ya_ref[r0:r0 + rc, :] = _silu(d * lax.rsqrt(var + 1e-5) * lg_ref[...] + lb_ref[...]).astype(BF16)

        t1 = (s * ts + r0 + 1 + lax.broadcasted_iota(jnp.int32, (rc, 1), 0)).astype(F32)
        run = ubuf[POOL_HALO + r0 - 8:POOL_HALO + r0 + rc, :]
        u0 = run[8:]
        totals = {}
        for dlt in range(1, 8):
            run = run + ubuf[POOL_HALO + r0 - 8 - dlt:POOL_HALO + r0 + rc - dlt, :]
            totals[dlt + 1] = run[8:]
        totals[16] = run[8:] + run[:rc]
        pooled = jnp.zeros((rc, GROUP), F32)
        for gi, w in enumerate(POOL_WINDOWS):
            pooled = jnp.where(lane_grp == gi, totals[w] / jnp.minimum(t1, float(w)) - u0, pooled)
        y = jnp.dot(pooled.astype(BF16), wp_ref[...], preferred_element_type=F32)
        yd_ref[r0:r0 + rc, :] = ((y + pb_ref[...]) * ps_ref[...]).astype(BF16)


def _local_mix(hc, up, cw, cb, lg, lb, wp, pb, ps, batch, seq):
    T, G = hc.shape
    ts = LOCAL_TILE
    ns = seq // ts
    row = pl.BlockSpec((ts, G), lambda b, s: (b * ns + s, 0))
    vec = _const_spec((1, G))
    return pl.pallas_call(
        _local_kernel,
        grid=(batch, ns),
        in_specs=[row, row, _const_spec(cw.shape), vec, vec, vec, _const_spec(wp.shape), vec, vec],
        out_specs=[row, row],
        out_shape=[jax.ShapeDtypeStruct((T, G), BF16)] * 2,
        scratch_shapes=[pltpu.VMEM((CONV_HALO + ts, G), F32), pltpu.VMEM((POOL_HALO + ts, G), F32)],
        compiler_params=_cparams(("arbitrary", "arbitrary")),
        name="local_mix",
    )(hc, up, cw, cb, lg, lb, wp, pb, ps)


def _split_dot(x, p):
    hi = x.astype(BF16)
    lo = (x - hi.astype(F32)).astype(BF16)
    return (jnp.dot(hi, p, preferred_element_type=F32) + jnp.dot(lo, p, preferred_element_type=F32))


def _retention_kernel(q_ref, k_ref, v_ref, g_ref, dm_ref, qd_ref, kd_ref, cd_ref, bd_ref, gg_ref, gb_ref,
                      y_ref, state):
    @pl.when(pl.program_id(1) == 0)
    def _():
        state[...] = jnp.zeros_like(state)

    q, k, v = q_ref[...], k_ref[...], v_ref[...]
    head = _lane_head((1, GROUP), 1)
    nt = (((1,), (1,)), ((), ()))
    o = jnp.dot(q, state[...].astype(BF16), preferred_element_type=F32) * qd_ref[...]
    for h in range(N_HEADS):
        mh = head == h
        qh = jnp.where(mh, q, jnp.zeros_like(q))
        sc = lax.dot_general(qh, k, nt, preferred_element_type=F32) * dm_ref[h]
        oh = jnp.dot(sc.astype(BF16), v, preferred_element_type=F32)
        o = o + jnp.where(mh, oh, 0.0)
    kdec = (k.astype(F32) * kd_ref[...]).astype(BF16)
    upd = lax.dot_general(kdec, v, (((0,), (0,)), ((), ())), preferred_element_type=F32)
    bd = bd_ref[...]
    state[...] = state[...] * cd_ref[...] + bd * upd

    p = (bd * (1.0 / HEAD_DIM)).astype(BF16)
    mu = _split_dot(o, p)
    d = o - mu
    var = _split_dot(d * d, p)
    on = d * lax.rsqrt(var + 1e-5) * gg_ref[...] + gb_ref[...]
    y_ref[...] = (_silu(g_ref[...]) * on).astype(BF16)


def _retention(rq, rk, rv, rg, rtabs, gg, gb, batch, seq):
    T, G = rq.shape
    C = RET_CHUNK
    nc = seq // C
    row = pl.BlockSpec((C, G), lambda b, c: (b * nc + c, 0))
    dm, qd, kd, cd, bd = rtabs
    vec = _const_spec((1, G))
    return pl.pallas_call(
        _retention_kernel,
        grid=(batch, nc),
        in_specs=[row, row, row, row, _const_spec(dm.shape), _const_spec(qd.shape), _const_spec(kd.shape),
                  vec, _const_spec(bd.shape), vec, vec],
        out_specs=row,
        out_shape=jax.ShapeDtypeStruct((T, G), BF16),
        scratch_shapes=[pltpu.VMEM((G, G), F32)],
        compiler_params=_cparams(("arbitrary", "arbitrary")),
        name="retention",
    )(rq, rk, rv, rg, dm, qd, kd, cd, bd, gg, gb)


def _moba_kernel(qt_ref, k_ref, vt_ref, km_ref, y_ref, sel_ref, qh_ref, m_ref, acc_ref, ot_ref, sa_ref, sb_ref):
    n = pl.program_id(1)
    nb = km_ref.shape[1]
    Q = MOBA_BLOCK
    qt = qt_ref[0]
    qs = (qt * (HEAD_DIM ** -0.5 * LOG2E)).astype(BF16)
    ch_head = _lane_head((GROUP, 1), 0)
    km = km_ref[0]
    km_head = _lane_head((1, GROUP), 1)
    blk = lax.broadcasted_iota(jnp.int32, (nb, Q), 0)
    past = blk < n

    km_heads = jnp.concatenate([jnp.where(km_head == h, km, 0.0) for h in range(N_HEADS)], axis=0)
    gate_all = jnp.dot(km_heads, qt, preferred_element_type=F32, precision=lax.Precision.HIGHEST)
    for h in range(N_HEADS):
        gate = jnp.where(past, gate_all[h * nb:(h + 1) * nb], -jnp.inf)
        sel = jnp.zeros((nb, Q), F32)
        for _ in range(MOBA_TOPK):
            top = jnp.max(gate, axis=0, keepdims=True)
            first = jnp.min(jnp.where(gate == top, blk, nb), axis=0, keepdims=True)
            pick = (blk == first) & past
            sel = jnp.where(pick, 1.0, sel)
            gate = jnp.where(pick, -jnp.inf, gate)
        sel_ref[h * nb:(h + 1) * nb, :] = sel
        qh_ref[h] = jnp.where(ch_head == h, qs, jnp.zeros_like(qs))

    m_ref[...] = jnp.full(m_ref.shape, -BIG, F32)
    acc_ref[...] = jnp.zeros(acc_ref.shape, F32)

    def score(j, s_ref):
        kb = k_ref[j]
        for h in range(N_HEADS):
            s_ref[h] = jnp.dot(kb, qh_ref[h], preferred_element_type=F32)

    def absorb(j, s_ref, own):
        causal = None
        if own is not False:
            causal = lax.broadcasted_iota(jnp.int32, (Q, Q), 0) <= lax.broadcasted_iota(jnp.int32, (Q, Q), 1)
            if own is not True:
                causal = jnp.logical_or(causal, jnp.logical_not(own))
        for h in range(N_HEADS):
            m = m_ref[h:h + 1, :]
            s = s_ref[h]
            if causal is not None:
                s = jnp.where(causal, s, NEG)
            bmax = jnp.max(s, axis=0, keepdims=True)
            if own is True:
                m_new = jnp.maximum(m, bmax)
                shift = m_new
            else:
                on = sel_ref[pl.ds(h * nb + j, 1), :] > 0.0
                if own is not False:
                    on = jnp.logical_or(on, own)
                m_new = jnp.where(on, jnp.maximum(m, bmax), m)
                shift = jnp.where(on, m_new, BIG)
            p = jnp.exp2(s - shift).astype(BF16)
            vt = vt_ref[j, h * VT_ROWS:(h + 1) * VT_ROWS, :]
            acc_ref[h] = jnp.exp2(m - m_new) * acc_ref[h] + jnp.dot(vt, p, preferred_element_type=F32)
            m_ref[h:h + 1, :] = m_new

    def body(i, carry):
        score(2 * i + 1, sb_ref)
        absorb(2 * i, sa_ref, False)
        score(2 * i + 2, sa_ref)
        absorb(2 * i + 1, sb_ref, False)
        return carry

    pairs = n // 2
    odd = n % 2 == 1
    score(0, sa_ref)
    lax.fori_loop(0, pairs, body, 0)

    @pl.when(odd)
    def _():
        score(n, sb_ref)

    absorb(2 * pairs, sa_ref, jnp.logical_not(odd))

    @pl.when(odd)
    def _():
        absorb(n, sb_ref, True)

    for h in range(N_HEADS):
        a = acc_ref[h]
        ot_ref[h * HEAD_DIM:(h + 1) * HEAD_DIM, :] = a[:HEAD_DIM] / a[HEAD_DIM:HEAD_DIM + 1]
    y_ref[...] = ot_ref[...].T.astype(BF16)


def _moba(mqt, mk3, mvt, km, batch, seq):
    nb = seq // MOBA_BLOCK
    B = MOBA_BLOCK
    G = GROUP
    return pl.pallas_call(
        _moba_kernel,
        grid=(batch, nb),
        in_specs=[pl.BlockSpec((1, G, B), lambda b, i: (b * nb + i, 0, 0)),
                  pl.BlockSpec((nb, B, G), lambda b, i: (b, 0, 0)),
                  pl.BlockSpec((nb, N_HEADS * VT_ROWS, B), lambda b, i: (b, 0, 0)),
                  pl.BlockSpec((1, nb, G), lambda b, i: (b, 0, 0))],
        out_specs=pl.BlockSpec((B, G), lambda b, i: (b * nb + i, 0)),
        out_shape=jax.ShapeDtypeStruct((batch * seq, G), BF16),
        scratch_shapes=[pltpu.VMEM((N_HEADS * nb, B), F32),
                        pltpu.VMEM((N_HEADS, G, B), BF16),
                        pltpu.VMEM((8, B), F32),
                        pltpu.VMEM((N_HEADS, VT_ROWS, B), F32),
                        pltpu.VMEM((G, B), F32),
                        pltpu.VMEM((N_HEADS, B, B), F32),
                        pltpu.VMEM((N_HEADS, B, B), F32)],
        compiler_params=_cparams(("arbitrary", "arbitrary")),
        name="moba",
    )(mqt, mk3, mvt, km)


def _post_kernel(x_ref, ya_ref, yb_ref, yc_ref, yd_ref, wo_ref, gpo_ref, gfi_ref,
                 wg_ref, wu_ref, wd_ref, gfo_ref, o_ref, f_scr):
    G = GROUP
    y = jnp.dot(ya_ref[...], wo_ref[0:G, :], preferred_element_type=F32)
    y = y + jnp.dot(yb_ref[...], wo_ref[G:2 * G, :], preferred_element_type=F32)
    y = y + jnp.dot(yc_ref[...], wo_ref[2 * G:3 * G, :], preferred_element_type=F32)
    y = y + jnp.dot(yd_ref[...], wo_ref[3 * G:4 * G, :], preferred_element_type=F32)
    x1 = x_ref[...] + _rms(y, gpo_ref[...])
    hb = _rms(x1, gfi_ref[...]).astype(BF16)
    d_ff = wg_ref.shape[1]
    for c in range(0, d_ff, FF_CHUNK):
        cols = slice(c, c + FF_CHUNK)
        gt = jnp.dot(hb, wg_ref[:, cols], preferred_element_type=F32)
        up = jnp.dot(hb, wu_ref[:, cols], preferred_element_type=F32)
        f_scr[:, cols] = (_silu(gt) * up).astype(BF16)
    f = jnp.dot(f_scr[...], wd_ref[...], preferred_element_type=F32)
    o_ref[...] = x1 + _rms(f, gfo_ref[...])


def _post(x2, ya, yb, yc, yd, wo, gpo, gfi, wg, wu, wd, gfo):
    T, D = x2.shape
    tm = ROW_TILE
    G = GROUP
    d_ff = wg.shape[1]
    row = lambda i: (i, 0)
    xs = pl.BlockSpec((tm, D), row)
    ys = pl.BlockSpec((tm, G), row)
    vec = _const_spec((1, D))
    return pl.pallas_call(
        _post_kernel,
        grid=(T // tm,),
        in_specs=[xs, ys, ys, ys, ys, _const_spec(wo.shape), vec, vec,
                  _const_spec(wg.shape), _const_spec(wu.shape), _const_spec(wd.shape), vec],
        out_specs=xs,
        out_shape=jax.ShapeDtypeStruct((T, D), F32),
        scratch_shapes=[pltpu.VMEM((tm, d_ff), BF16)],
        compiler_params=_cparams(("arbitrary",)),
        name="out_proj_ffn",
    )(x2, ya, yb, yc, yd, wo, gpo, gfi, wg, wu, wd, gfo)


def _rotary_tables(seq, rot_dim, theta):
    pos = jnp.arange(seq, dtype=F32)
    inv = 1.0 / (theta ** (jnp.arange(0, rot_dim, 2, dtype=F32) / rot_dim))
    ang = pos[:, None] * inv[None, :]
    cos, sin = jnp.cos(ang), jnp.sin(ang)
    rest = HEAD_DIM - rot_dim
    cos_h = jnp.concatenate([cos, cos, jnp.ones((seq, rest), F32)], axis=-1)
    sin_h = jnp.concatenate([-sin, sin, jnp.zeros((seq, rest), F32)], axis=-1)
    return jnp.tile(cos_h, (1, N_HEADS)), jnp.tile(sin_h, (1, N_HEADS))


def _retention_tables():
    c = RET_CHUNK
    log_gamma = jnp.log(1.0 - 2.0 ** (-5.0 - jnp.arange(N_HEADS, dtype=F32)))
    idx = jnp.arange(c, dtype=F32)
    diff = idx[:, None] - idx[None, :]
    dm = jnp.where(diff[None] >= 0, jnp.exp(jnp.maximum(diff, 0.0)[None] * log_gamma[:, None, None]), 0.0)
    lanes = lambda t: jnp.repeat(t, HEAD_DIM, axis=-1)
    qd = lanes(jnp.exp((idx + 1.0)[:, None] * log_gamma[None, :]))
    kd = lanes(jnp.exp((c - 1.0 - idx)[:, None] * log_gamma[None, :]))
    cd = lanes(jnp.exp(c * log_gamma)[None, :])
    head = jnp.arange(GROUP) // HEAD_DIM
    bd = (head[:, None] == head[None, :]).astype(F32)
    return dm, qd, kd, cd, bd


def _block_diag(w):
    n, c, _ = w.shape
    eye = jnp.eye(n, dtype=w.dtype)
    return (eye[:, None, :, None] * w[:, :, None, :]).reshape(n * c, n * c)


def kernel(x, attn_pre_g, w_in, conv_w, conv_b, conv_ln_g, conv_ln_b, ret_gn_g, ret_gn_b, pool_w, pool_b,
           pool_scale, w_out, attn_post_g, ffn_pre_g, w_gate, w_up, w_down, ffn_post_g):
    batch, seq, d_model = x.shape
    depth = w_in.shape[0]
    assert seq % ROW_TILE == 0 and seq % LOCAL_TILE == 0 and seq % RET_CHUNK == 0 and seq % MOBA_BLOCK == 0
    nb = seq // MOBA_BLOCK
    tabs = _rotary_tables(seq, HEAD_DIM, RET_ROT_THETA) + _rotary_tables(seq, ROT_DIM, ROPE_THETA)
    rtabs = _retention_tables()
    vec = lambda a: a.reshape(1, -1)

    x2 = x.reshape(batch * seq, d_model)
    for l in range(depth):
        hc, rq, rk, rv, rg, mqt, mk, mvt, km, up = _in_proj(
            x2, vec(attn_pre_g[l]), w_in[l].astype(BF16), tabs, seq)
        ya, yd = _local_mix(hc, up, conv_w[l], vec(conv_b[l]), vec(conv_ln_g[l]), vec(conv_ln_b[l]),
                            _block_diag(pool_w[l]).astype(BF16), vec(pool_b[l]), vec(pool_scale[l]), batch, seq)
        yb = _retention(rq, rk, rv, rg, rtabs, vec(ret_gn_g[l]), vec(ret_gn_b[l]), batch, seq)
        yc = _moba(mqt, mk.reshape(batch * nb, MOBA_BLOCK, GROUP), mvt, km.reshape(batch, nb, GROUP), batch, seq)
        x2 = _post(x2, ya, yb, yc, yd, w_out[l].astype(BF16), vec(attn_post_g[l]), vec(ffn_pre_g[l]),
                   w_gate[l].astype(BF16), w_up[l].astype(BF16), w_down[l].astype(BF16), vec(ffn_post_g[l]))
    return x2.reshape(batch, seq, d_model)
```

```python
import functools
import math

import jax
import jax.numpy as jnp
from jax import lax
from jax.experimental import pallas as pl
from jax.experimental.pallas import tpu as pltpu

F32 = jnp.float32
BF16 = jnp.bfloat16

GROUP = 256
HEAD_DIM = 64
N_HEADS = GROUP // HEAD_DIM
CONV_WIDTH = 31
RET_ROT_THETA = 10000.0
MOBA_BLOCK = 256
MOBA_TOPK = 3
ROPE_THETA = 500000.0
ROT_DIM = HEAD_DIM // 4
POOL_WINDOWS = (2, 4, 8, 16)
NEG = -1e30
BIG = 1e30
VT_ROWS = HEAD_DIM + 16
LOG2E = math.log2(math.e)

ROW_TILE = 512
LOCAL_TILE = 512
LOCAL_CHUNK = 64
CONV_HALO = 32
POOL_HALO = 16
RET_CHUNK = 256
FF_CHUNK = 256
VMEM_LIMIT = 56 * 1024 * 1024


def _cparams(sem):
    return pltpu.CompilerParams(dimension_semantics=sem, vmem_limit_bytes=VMEM_LIMIT)


def _const_spec(shape):
    nd = len(shape)
    return pl.BlockSpec(shape, lambda *_: (0,) * nd, pipeline_mode=pl.Buffered(1))


def _rms(x, g):
    return x * lax.rsqrt(jnp.mean(x * x, axis=-1, keepdims=True) + 1e-6) * g


def _silu(x):
    return x * jax.nn.sigmoid(x)


def _lane_head(shape, axis):
    return lax.broadcasted_iota(jnp.int32, shape, axis) // HEAD_DIM


def _rotate_half(x, cos, sin_signed, half):
    n = x.shape[-1]
    lane = lax.broadcasted_iota(jnp.int32, x.shape, 1) % HEAD_DIM
    partner = jnp.where(lane < half, pltpu.roll(x, n - half, 1), pltpu.roll(x, half, 1))
    return x * cos + partner * sin_signed


def _in_proj_kernel(x_ref, g_ref, w_ref, cr_ref, sr_ref, cm_ref, sm_ref,
                    hc_ref, rq_ref, rk_ref, rv_ref, rg_ref, mqt_ref, mk_ref, mvt_ref, km_ref, up_ref):
    G = GROUP
    hb = _rms(x_ref[...], g_ref[...]).astype(BF16)

    def proj(i, n=1):
        return jnp.dot(hb, w_ref[:, i * G:(i + n) * G], preferred_element_type=F32)

    uc = proj(0, 2)
    hc_ref[...] = uc[:, :G] * jax.nn.sigmoid(uc[:, G:])
    cr, sr = cr_ref[...], sr_ref[...]
    rq_ref[...] = _rotate_half(proj(2), cr, sr, HEAD_DIM // 2).astype(BF16)
    rk_ref[...] = (_rotate_half(proj(3), cr, sr, HEAD_DIM // 2) * (HEAD_DIM ** -0.5)).astype(BF16)
    rv_ref[...] = proj(4).astype(BF16)
    rg_ref[...] = proj(5)
    cm, sm = cm_ref[...], sm_ref[...]
    mq = _rotate_half(proj(6), cm, sm, ROT_DIM // 2)
    mk = _rotate_half(proj(7), cm, sm, ROT_DIM // 2)
    mv = proj(8)
    mk_ref[...] = mk.astype(BF16)
    nblk = mq.shape[0] // MOBA_BLOCK
    for j in range(nblk):
        rows = slice(j * MOBA_BLOCK, (j + 1) * MOBA_BLOCK)
        mqt_ref[j] = mq[rows].T
        vt = mv[rows].T.astype(BF16)
        for h in range(N_HEADS):
            r = h * VT_ROWS
            mvt_ref[j, r:r + HEAD_DIM, :] = vt[h * HEAD_DIM:(h + 1) * HEAD_DIM]
            mvt_ref[j, r + HEAD_DIM:r + VT_ROWS, :] = jnp.ones((VT_ROWS - HEAD_DIM, MOBA_BLOCK), BF16)
        km_ref[0, j:j + 1, :] = jnp.mean(mk[rows], axis=0, keepdims=True)
    up_ref[...] = proj(9)


def _in_proj(x2, g, w_in, tabs, seq):
    T, D = x2.shape
    tm = ROW_TILE
    nt = T // tm
    per_seq = seq // tm
    nblk = tm // MOBA_BLOCK
    G = GROUP
    row = lambda i: (i, 0)
    tab = pl.BlockSpec((tm, G), lambda i: (i % per_seq, 0))
    f32o = jax.ShapeDtypeStruct((T, G), F32)
    bfo = jax.ShapeDtypeStruct((T, G), BF16)
    blk3 = lambda rows, dt: jax.ShapeDtypeStruct((T // MOBA_BLOCK, rows, MOBA_BLOCK), dt)
    o_row = pl.BlockSpec((tm, G), row)
    o_blk = lambda rows: pl.BlockSpec((nblk, rows, MOBA_BLOCK), lambda i: (i, 0, 0))
    return pl.pallas_call(
        _in_proj_kernel,
        grid=(nt,),
        in_specs=[pl.BlockSpec((tm, D), row), _const_spec((1, D)), _const_spec(w_in.shape),
                  tab, tab, tab, tab],
        out_specs=[o_row, o_row, o_row, o_row, o_row, o_blk(G), o_row, o_blk(N_HEADS * VT_ROWS),
                   pl.BlockSpec((1, nblk, G), lambda i: (i, 0, 0)), o_row],
        out_shape=[f32o, bfo, bfo, bfo, f32o, blk3(G, F32), bfo, blk3(N_HEADS * VT_ROWS, BF16),
                   jax.ShapeDtypeStruct((nt, nblk, G), F32), f32o],
        compiler_params=_cparams(("arbitrary",)),
        name="in_proj",
    )(x2, g, w_in, *tabs)


def _local_kernel(hc_ref, up_ref, cw_ref, cb_ref, lg_ref, lb_ref, wp_ref, pb_ref, ps_ref,
                  ya_ref, yd_ref, hbuf, ubuf):
    ts = LOCAL_TILE
    s = pl.program_id(1)

    @pl.when(s == 0)
    def _():
        hbuf[0:CONV_HALO, :] = jnp.zeros((CONV_HALO, GROUP), F32)
        ubuf[0:POOL_HALO, :] = jnp.zeros((POOL_HALO, GROUP), F32)

    @pl.when(s > 0)
    def _():
        hbuf[0:CONV_HALO, :] = hbuf[ts:ts + CONV_HALO, :]
        ubuf[0:POOL_HALO, :] = ubuf[ts:ts + POOL_HALO, :]

    hbuf[CONV_HALO:, :] = hc_ref[...]
    ubuf[POOL_HALO:, :] = up_ref[...]

    rc = LOCAL_CHUNK
    lane_grp = lax.broadcasted_iota(jnp.int32, (rc, GROUP), 1) // (GROUP // len(POOL_WINDOWS))
    for c in range(ts // rc):
        r0 = c * rc
        back = 8 * ((CONV_WIDTH - 1) // 8)
        acc = jnp.broadcast_to(cb_ref[...], (rc, GROUP))
        for r in range(8):
            lo = CONV_HALO + r0 - back - r
            hr = hbuf[lo:lo + back + rc, :]
            part = None
            for q in range(back // 8 + 1):
                d = 8 * q + r
                if d < CONV_WIDTH:
                    j = CONV_WIDTH - 1 - d
                    term = cw_ref[j:j + 1, :] * hr[back - 8 * q:back - 8 * q + rc]
                    part = term if part is None else part + term
            acc = acc + part
        mu = jnp.mean(acc, axis=-1, keepdims=True)
        d = acc - mu
        var = jnp.mean(d * d, axis=-1, keepdims=True)
        ya_ref[r0:r0 + rc, :] = _silu(d * lax.rsqrt(var + 1e-5) * lg_ref[...] + lb_ref[...]).astype(BF16)

        t1 = (s * ts + r0 + 1 + lax.broadcasted_iota(jnp.int32, (rc, 1), 0)).astype(F32)
        run = ubuf[POOL_HALO + r0 - 8:POOL_HALO + r0 + rc, :]
        u0 = run[8:]
        totals = {}
        for dlt in range(1, 8):
            run = run + ubuf[POOL_HALO + r0 - 8 - dlt:POOL_HALO + r0 + rc - dlt, :]
            totals[dlt + 1] = run[8:]
        totals[16] = run[8:] + run[:rc]
        pooled = jnp.zeros((rc, GROUP), F32)
        for gi, w in enumerate(POOL_WINDOWS):
            pooled = jnp.where(lane_grp == gi, totals[w] / jnp.minimum(t1, float(w)) - u0, pooled)
        y = jnp.dot(pooled.astype(BF16), wp_ref[...], preferred_element_type=F32)
        yd_ref[r0:r0 + rc, :] = ((y + pb_ref[...]) * ps_ref[...]).astype(BF16)


def _local_mix(hc, up, cw, cb, lg, lb, wp, pb, ps, batch, seq):
    T, G = hc.shape
    ts = LOCAL_TILE
    ns = seq // ts
    row = pl.BlockSpec((ts, G), lambda b, s: (b * ns + s, 0))
    vec = _const_spec((1, G))
    return pl.pallas_call(
        _local_kernel,
        grid=(batch, ns),
        in_specs=[row, row, _const_spec(cw.shape), vec, vec, vec, _const_spec(wp.shape), vec, vec],
        out_specs=[row, row],
        out_shape=[jax.ShapeDtypeStruct((T, G), BF16)] * 2,
        scratch_shapes=[pltpu.VMEM((CONV_HALO + ts, G), F32), pltpu.VMEM((POOL_HALO + ts, G), F32)],
        compiler_params=_cparams(("arbitrary", "arbitrary")),
        name="local_mix",
    )(hc, up, cw, cb, lg, lb, wp, pb, ps)


def _split_dot(x, p):
    hi = x.astype(BF16)
    lo = (x - hi.astype(F32)).astype(BF16)
    return (jnp.dot(hi, p, preferred_element_type=F32) + jnp.dot(lo, p, preferred_element_type=F32))


def _retention_kernel(q_ref, k_ref, v_ref, g_ref, dm_ref, qd_ref, kd_ref, cd_ref, bd_ref, gg_ref, gb_ref,
                      y_ref, state):
    @pl.when(pl.program_id(1) == 0)
    def _():
        state[...] = jnp.zeros_like(state)

    q, k, v = q_ref[...], k_ref[...], v_ref[...]
    head = _lane_head((1, GROUP), 1)
    nt = (((1,), (1,)), ((), ()))
    masks = [head == h for h in range(N_HEADS)]
    scores = [lax.dot_general(jnp.where(mh, q, jnp.zeros_like(q)), k, nt, preferred_element_type=F32)
              for mh in masks]
    o = jnp.dot(q, state[...].astype(BF16), preferred_element_type=F32) * qd_ref[...]
    kdec = (k.astype(F32) * kd_ref[...]).astype(BF16)
    upd = lax.dot_general(kdec, v, (((0,), (0,)), ((), ())), preferred_element_type=F32)
    bd = bd_ref[...]
    state[...] = state[...] * cd_ref[...] + bd * upd
    for h, mh in enumerate(masks):
        oh = jnp.dot((scores[h] * dm_ref[h]).astype(BF16), v, preferred_element_type=F32)
        o = o + jnp.where(mh, oh, 0.0)

    p = (bd * (1.0 / HEAD_DIM)).astype(BF16)
    mu = _split_dot(o, p)
    d = o - mu
    var = _split_dot(d * d, p)
    on = d * lax.rsqrt(var + 1e-5) * gg_ref[...] + gb_ref[...]
    y_ref[...] = (_silu(g_ref[...]) * on).astype(BF16)


def _retention(rq, rk, rv, rg, rtabs, gg, gb, batch, seq):
    T, G = rq.shape
    C = RET_CHUNK
    nc = seq // C
    row = pl.BlockSpec((C, G), lambda b, c: (b * nc + c, 0))
    dm, qd, kd, cd, bd = rtabs
    vec = _const_spec((1, G))
    return pl.pallas_call(
        _retention_kernel,
        grid=(batch, nc),
        in_specs=[row, row, row, row, _const_spec(dm.shape), _const_spec(qd.shape), _const_spec(kd.shape),
                  vec, _const_spec(bd.shape), vec, vec],
        out_specs=row,
        out_shape=jax.ShapeDtypeStruct((T, G), BF16),
        scratch_shapes=[pltpu.VMEM((G, G), F32)],
        compiler_params=_cparams(("arbitrary", "arbitrary")),
        name="retention",
    )(rq, rk, rv, rg, dm, qd, kd, cd, bd, gg, gb)


MOBA_TILES = 2


def _moba_kernel(qt_ref, k_ref, vt_ref, km_ref, y_ref, sel_ref, qh_ref, m_ref, acc_ref, ot_ref, sa_ref, sb_ref):
    a = pl.program_id(1)
    nb = km_ref.shape[1]
    Q = MOBA_BLOCK
    chains = [(t, h) for t in range(MOBA_TILES) for h in range(N_HEADS)]
    cid = lambda t, h: t * N_HEADS + h
    ch_head = _lane_head((GROUP, 1), 0)
    km = km_ref[0]
    km_head = _lane_head((1, GROUP), 1)
    blk = lax.broadcasted_iota(jnp.int32, (nb, Q), 0)

    km_heads = jnp.concatenate([jnp.where(km_head == h, km, 0.0) for h in range(N_HEADS)], axis=0)
    for t in range(MOBA_TILES):
        qt = qt_ref[t]
        qs = (qt * (HEAD_DIM ** -0.5 * LOG2E)).astype(BF16)
        gate_all = jnp.dot(km_heads, qt, preferred_element_type=F32, precision=lax.Precision.HIGHEST)
        past = blk < MOBA_TILES * a + t
        for h in range(N_HEADS):
            gate = jnp.where(past, gate_all[h * nb:(h + 1) * nb], -jnp.inf)
            sel = jnp.zeros((nb, Q), F32)
            for _ in range(MOBA_TOPK):
                top = jnp.max(gate, axis=0, keepdims=True)
                first = jnp.min(jnp.where(gate == top, blk, nb), axis=0, keepdims=True)
                pick = (blk == first) & past
                sel = jnp.where(pick, 1.0, sel)
                gate = jnp.where(pick, -jnp.inf, gate)
            sel_ref[cid(t, h) * nb:(cid(t, h) + 1) * nb, :] = sel
            qh_ref[cid(t, h)] = jnp.where(ch_head == h, qs, jnp.zeros_like(qs))

    m_ref[...] = jnp.full(m_ref.shape, -BIG, F32)
    acc_ref[...] = jnp.zeros(acc_ref.shape, F32)

    def score(j, s_ref, which):
        kb = k_ref[j]
        for t, h in which:
            s_ref[cid(t, h)] = jnp.dot(kb, qh_ref[cid(t, h)], preferred_element_type=F32)

    def absorb(j, s_ref, which, own_tile=None):
        causal = None
        for t, h in which:
            c = cid(t, h)
            m = m_ref[c:c + 1, :]
            s = s_ref[c]
            if t == own_tile:
                if causal is None:
                    causal = (lax.broadcasted_iota(jnp.int32, (Q, Q), 0)
                              <= lax.broadcasted_iota(jnp.int32, (Q, Q), 1))
                s = jnp.where(causal, s, NEG)
                m_new = jnp.maximum(m, jnp.max(s, axis=0, keepdims=True))
                shift = m_new
            else:
                on = sel_ref[pl.ds(c * nb + j, 1), :] > 0.0
                m_new = jnp.where(on, jnp.maximum(m, jnp.max(s, axis=0, keepdims=True)), m)
                shift = jnp.where(on, m_new, BIG)
            p = jnp.exp2(s - shift).astype(BF16)
            vt = vt_ref[j, h * VT_ROWS:(h + 1) * VT_ROWS, :]
            acc_ref[c] = jnp.exp2(m - m_new) * acc_ref[c] + jnp.dot(vt, p, preferred_element_type=F32)
            m_ref[c:c + 1, :] = m_new

    def body(i, carry):
        score(2 * i + 1, sb_ref, chains)
        absorb(2 * i, sa_ref, chains)
        score(2 * i + 2, sa_ref, chains)
        absorb(2 * i + 1, sb_ref, chains)
        return carry

    first = MOBA_TILES * a
    second = [(1, h) for h in range(N_HEADS)]
    score(0, sa_ref, chains)
    lax.fori_loop(0, a, body, 0)
    score(first + 1, sb_ref, second)
    absorb(first, sa_ref, chains, own_tile=0)
    absorb(first + 1, sb_ref, second, own_tile=1)

    for t in range(MOBA_TILES):
        for h in range(N_HEADS):
            acc = acc_ref[cid(t, h)]
            ot_ref[h * HEAD_DIM:(h + 1) * HEAD_DIM, :] = acc[:HEAD_DIM] / acc[HEAD_DIM:HEAD_DIM + 1]
        y_ref[t * Q:(t + 1) * Q, :] = ot_ref[...].T.astype(BF16)


def _moba(mqt, mk3, mvt, km, batch, seq):
    nb = seq // MOBA_BLOCK
    B = MOBA_BLOCK
    G = GROUP
    nt = MOBA_TILES
    assert nt == 2 and nb % nt == 0
    steps = nb // nt
    nc = nt * N_HEADS
    return pl.pallas_call(
        _moba_kernel,
        grid=(batch, steps),
        in_specs=[pl.BlockSpec((nt, G, B), lambda b, i: (b * steps + i, 0, 0)),
                  pl.BlockSpec((nb, B, G), lambda b, i: (b, 0, 0)),
                  pl.BlockSpec((nb, N_HEADS * VT_ROWS, B), lambda b, i: (b, 0, 0)),
                  pl.BlockSpec((1, nb, G), lambda b, i: (b, 0, 0))],
        out_specs=pl.BlockSpec((nt * B, G), lambda b, i: (b * steps + i, 0)),
        out_shape=jax.ShapeDtypeStruct((batch * seq, G), BF16),
        scratch_shapes=[pltpu.VMEM((nc * nb, B), F32),
                        pltpu.VMEM((nc, G, B), BF16),
                        pltpu.VMEM((nc, B), F32),
                        pltpu.VMEM((nc, VT_ROWS, B), F32),
                        pltpu.VMEM((G, B), F32),
                        pltpu.VMEM((nc, B, B), F32),
                        pltpu.VMEM((nc, B, B), F32)],
        compiler_params=_cparams(("arbitrary", "arbitrary")),
        name="moba",
    )(mqt, mk3, mvt, km)


def _post_kernel(x_ref, ya_ref, yb_ref, yc_ref, yd_ref, wo_ref, gpo_ref, gfi_ref,
                 wg_ref, wu_ref, wd_ref, gfo_ref, o_ref, f_scr):
    G = GROUP
    y = jnp.dot(ya_ref[...], wo_ref[0:G, :], preferred_element_type=F32)
    y = y + jnp.dot(yb_ref[...], wo_ref[G:2 * G, :], preferred_element_type=F32)
    y = y + jnp.dot(yc_ref[...], wo_ref[2 * G:3 * G, :], preferred_element_type=F32)
    y = y + jnp.dot(yd_ref[...], wo_ref[3 * G:4 * G, :], preferred_element_type=F32)
    x1 = x_ref[...] + _rms(y, gpo_ref[...])
    hb = _rms(x1, gfi_ref[...]).astype(BF16)
    d_ff = wg_ref.shape[1]
    for c in range(0, d_ff, FF_CHUNK):
        cols = slice(c, c + FF_CHUNK)
        gt = jnp.dot(hb, wg_ref[:, cols], preferred_element_type=F32)
        up = jnp.dot(hb, wu_ref[:, cols], preferred_element_type=F32)
        f_scr[:, cols] = (_silu(gt) * up).astype(BF16)
    f = jnp.dot(f_scr[...], wd_ref[...], preferred_element_type=F32)
    o_ref[...] = x1 + _rms(f, gfo_ref[...])


def _post(x2, ya, yb, yc, yd, wo, gpo, gfi, wg, wu, wd, gfo):
    T, D = x2.shape
    tm = ROW_TILE
    G = GROUP
    d_ff = wg.shape[1]
    row = lambda i: (i, 0)
    xs = pl.BlockSpec((tm, D), row)
    ys = pl.BlockSpec((tm, G), row)
    vec = _const_spec((1, D))
    return pl.pallas_call(
        _post_kernel,
        grid=(T // tm,),
        in_specs=[xs, ys, ys, ys, ys, _const_spec(wo.shape), vec, vec,
                  _const_spec(wg.shape), _const_spec(wu.shape), _const_spec(wd.shape), vec],
        out_specs=xs,
        out_shape=jax.ShapeDtypeStruct((T, D), F32),
        scratch_shapes=[pltpu.VMEM((tm, d_ff), BF16)],
        compiler_params=_cparams(("arbitrary",)),
        name="out_proj_ffn",
    )(x2, ya, yb, yc, yd, wo, gpo, gfi, wg, wu, wd, gfo)


def _rotary_tables(seq, rot_dim, theta):
    pos = jnp.arange(seq, dtype=F32)
    inv = 1.0 / (theta ** (jnp.arange(0, rot_dim, 2, dtype=F32) / rot_dim))
    ang = pos[:, None] * inv[None, :]
    cos, sin = jnp.cos(ang), jnp.sin(ang)
    rest = HEAD_DIM - rot_dim
    cos_h = jnp.concatenate([cos, cos, jnp.ones((seq, rest), F32)], axis=-1)
    sin_h = jnp.concatenate([-sin, sin, jnp.zeros((seq, rest), F32)], axis=-1)
    return jnp.tile(cos_h, (1, N_HEADS)), jnp.tile(sin_h, (1, N_HEADS))


def _retention_tables():
    c = RET_CHUNK
    log_gamma = jnp.log(1.0 - 2.0 ** (-5.0 - jnp.arange(N_HEADS, dtype=F32)))
    idx = jnp.arange(c, dtype=F32)
    diff = idx[:, None] - idx[None, :]
    dm = jnp.where(diff[None] >= 0, jnp.exp(jnp.maximum(diff, 0.0)[None] * log_gamma[:, None, None]), 0.0)
    lanes = lambda t: jnp.repeat(t, HEAD_DIM, axis=-1)
    qd = lanes(jnp.exp((idx + 1.0)[:, None] * log_gamma[None, :]))
    kd = lanes(jnp.exp((c - 1.0 - idx)[:, None] * log_gamma[None, :]))
    cd = lanes(jnp.exp(c * log_gamma)[None, :])
    head = jnp.arange(GROUP) // HEAD_DIM
    bd = (head[:, None] == head[None, :]).astype(F32)
    return dm, qd, kd, cd, bd


def _block_diag(w):
    n, c, _ = w.shape
    eye = jnp.eye(n, dtype=w.dtype)
    return (eye[:, None, :, None] * w[:, :, None, :]).reshape(n * c, n * c)


def kernel(x, attn_pre_g, w_in, conv_w, conv_b, conv_ln_g, conv_ln_b, ret_gn_g, ret_gn_b, pool_w, pool_b,
           pool_scale, w_out, attn_post_g, ffn_pre_g, w_gate, w_up, w_down, ffn_post_g):
    batch, seq, d_model = x.shape
    depth = w_in.shape[0]
    assert seq % ROW_TILE == 0 and seq % LOCAL_TILE == 0 and seq % RET_CHUNK == 0 and seq % MOBA_BLOCK == 0
    nb = seq // MOBA_BLOCK
    tabs = _rotary_tables(seq, HEAD_DIM, RET_ROT_THETA) + _rotary_tables(seq, ROT_DIM, ROPE_THETA)
    rtabs = _retention_tables()
    vec = lambda a: a.reshape(1, -1)

    x2 = x.reshape(batch * seq, d_model)
    for l in range(depth):
        hc, rq, rk, rv, rg, mqt, mk, mvt, km, up = _in_proj(
            x2, vec(attn_pre_g[l]), w_in[l].astype(BF16), tabs, seq)
        ya, yd = _local_mix(hc, up, conv_w[l], vec(conv_b[l]), vec(conv_ln_g[l]), vec(conv_ln_b[l]),
                            _block_diag(pool_w[l]).astype(BF16), vec(pool_b[l]), vec(pool_scale[l]), batch, seq)
        yb = _retention(rq, rk, rv, rg, rtabs, vec(ret_gn_g[l]), vec(ret_gn_b[l]), batch, seq)
        yc = _moba(mqt, mk.reshape(batch * nb, MOBA_BLOCK, GROUP), mvt, km.reshape(batch, nb, GROUP), batch, seq)
        x2 = _post(x2, ya, yb, yc, yd, w_out[l].astype(BF16), vec(attn_post_g[l]), vec(ffn_pre_g[l]),
                   w_gate[l].astype(BF16), w_up[l].astype(BF16), w_down[l].astype(BF16), vec(ffn_post_g[l]))
    return x2.reshape(batch, seq, d_model)
```

```python
import functools
import math

import jax
import jax.numpy as jnp
from jax import lax
from jax.experimental import pallas as pl
from jax.experimental.pallas import tpu as pltpu

F32 = jnp.float32
BF16 = jnp.bfloat16

GROUP = 256
HEAD_DIM = 64
N_HEADS = GROUP // HEAD_DIM
CONV_WIDTH = 31
RET_ROT_THETA = 10000.0
MOBA_BLOCK = 256
MOBA_TOPK = 3
ROPE_THETA = 500000.0
ROT_DIM = HEAD_DIM // 4
POOL_WINDOWS = (2, 4, 8, 16)
NEG = -1e30
BIG = 1e30
VT_ROWS = HEAD_DIM + 16
LOG2E = math.log2(math.e)

ROW_TILE = 512
LOCAL_CHUNK = 64
CONV_HALO = 32
POOL_HALO = 16
RET_CHUNK = 256
RET_SEQS = 4
FF_CHUNK = 256
MOBA_TILES = 2
VMEM_LIMIT = 56 * 1024 * 1024


def _cparams(sem):
    return pltpu.CompilerParams(dimension_semantics=sem, vmem_limit_bytes=VMEM_LIMIT)


def _const_spec(shape):
    nd = len(shape)
    return pl.BlockSpec(shape, lambda *_: (0,) * nd, pipeline_mode=pl.Buffered(1))


def _rms(x, g):
    return x * lax.rsqrt(jnp.mean(x * x, axis=-1, keepdims=True) + 1e-6) * g


def _silu(x):
    return x * jax.nn.sigmoid(x)


def _lane_head(shape, axis):
    return lax.broadcasted_iota(jnp.int32, shape, axis) // HEAD_DIM


def _rotate_half(x, cos, sin_signed, half):
    n = x.shape[-1]
    lane = lax.broadcasted_iota(jnp.int32, x.shape, 1) % HEAD_DIM
    partner = jnp.where(lane < half, pltpu.roll(x, n - half, 1), pltpu.roll(x, half, 1))
    return x * cos + partner * sin_signed


def _shift_halo(buf, halo, ts, first):
    @pl.when(first)
    def _():
        buf[0:halo, :] = jnp.zeros((halo, GROUP), F32)

    @pl.when(jnp.logical_not(first))
    def _():
        buf[0:halo, :] = buf[ts:ts + halo, :]


def _conv_chunk(hbuf, r0, rc, cw_ref, cb_ref, lg_ref, lb_ref):
    back = 8 * ((CONV_WIDTH - 1) // 8)
    acc = jnp.broadcast_to(cb_ref[...], (rc, GROUP))
    for r in range(8):
        lo = CONV_HALO + r0 - back - r
        hr = hbuf[lo:lo + back + rc, :]
        part = None
        for q in range(back // 8 + 1):
            d = 8 * q + r
            if d < CONV_WIDTH:
                j = CONV_WIDTH - 1 - d
                term = cw_ref[j:j + 1, :] * hr[back - 8 * q:back - 8 * q + rc]
                part = term if part is None else part + term
        acc = acc + part
    mu = jnp.mean(acc, axis=-1, keepdims=True)
    d = acc - mu
    var = jnp.mean(d * d, axis=-1, keepdims=True)
    return _silu(d * lax.rsqrt(var + 1e-5) * lg_ref[...] + lb_ref[...])


def _pool_chunk(ubuf, r0, rc, t0):
    lane_grp = lax.broadcasted_iota(jnp.int32, (rc, GROUP), 1) // (GROUP // len(POOL_WINDOWS))
    t1 = (t0 + r0 + 1 + lax.broadcasted_iota(jnp.int32, (rc, 1), 0)).astype(F32)
    run = ubuf[POOL_HALO + r0 - 8:POOL_HALO + r0 + rc, :]
    u0 = run[8:]
    totals = {}
    for dlt in range(1, 8):
        run = run + ubuf[POOL_HALO + r0 - 8 - dlt:POOL_HALO + r0 + rc - dlt, :]
        totals[dlt + 1] = run[8:]
    totals[16] = run[8:] + run[:rc]
    pooled = jnp.zeros((rc, GROUP), F32)
    for gi, w in enumerate(POOL_WINDOWS):
        pooled = jnp.where(lane_grp == gi, totals[w] / jnp.minimum(t1, float(w)) - u0, pooled)
    return pooled


def _in_proj_kernel(per_seq, x_ref, g_ref, w_ref, cr_ref, sr_ref, cm_ref, sm_ref,
                    cw_ref, cb_ref, lg_ref, lb_ref, wp_ref, pb_ref, ps_ref,
                    ya_ref, rq_ref, rk_ref, rv_ref, rg_ref, mqt_ref, mk_ref, mvt_ref, km_ref, yd_ref,
                    hbuf, ubuf, pool_scr):
    G = GROUP
    ts = x_ref.shape[0]
    s = pl.program_id(0) % per_seq
    _shift_halo(hbuf, CONV_HALO, ts, s == 0)
    _shift_halo(ubuf, POOL_HALO, ts, s == 0)
    hb = _rms(x_ref[...], g_ref[...]).astype(BF16)

    def proj(i, n=1):
        return jnp.dot(hb, w_ref[:, i * G:(i + n) * G], preferred_element_type=F32)

    uc = proj(0, 2)
    hbuf[CONV_HALO:, :] = uc[:, :G] * jax.nn.sigmoid(uc[:, G:])
    ubuf[POOL_HALO:, :] = proj(9)
    cr, sr = cr_ref[...], sr_ref[...]
    rq_ref[...] = _rotate_half(proj(2), cr, sr, HEAD_DIM // 2).astype(BF16)
    rk_ref[...] = (_rotate_half(proj(3), cr, sr, HEAD_DIM // 2) * (HEAD_DIM ** -0.5)).astype(BF16)
    rv_ref[...] = proj(4).astype(BF16)
    rg_ref[...] = proj(5)
    cm, sm = cm_ref[...], sm_ref[...]
    mq = _rotate_half(proj(6), cm, sm, ROT_DIM // 2)
    mk = _rotate_half(proj(7), cm, sm, ROT_DIM // 2)
    mv = proj(8)
    mk_ref[...] = mk.astype(BF16)
    nblk = mq.shape[0] // MOBA_BLOCK
    for j in range(nblk):
        rows = slice(j * MOBA_BLOCK, (j + 1) * MOBA_BLOCK)
        mqt_ref[j] = mq[rows].T
        vt = mv[rows].T.astype(BF16)
        for h in range(N_HEADS):
            r = h * VT_ROWS
            mvt_ref[j, r:r + HEAD_DIM, :] = vt[h * HEAD_DIM:(h + 1) * HEAD_DIM]
            mvt_ref[j, r + HEAD_DIM:r + VT_ROWS, :] = jnp.ones((VT_ROWS - HEAD_DIM, MOBA_BLOCK), BF16)
        km_ref[0, j:j + 1, :] = jnp.mean(mk[rows], axis=0, keepdims=True)

    rc = LOCAL_CHUNK
    for r0 in range(0, ts, rc):
        ya_ref[r0:r0 + rc, :] = _conv_chunk(hbuf, r0, rc, cw_ref, cb_ref, lg_ref, lb_ref).astype(BF16)
        pool_scr[r0:r0 + rc, :] = _pool_chunk(ubuf, r0, rc, s * ts).astype(BF16)
    y = jnp.dot(pool_scr[...], wp_ref[...], preferred_element_type=F32)
    yd_ref[...] = ((y + pb_ref[...]) * ps_ref[...]).astype(BF16)


def _in_proj(x2, g, w_in, tabs, local, seq):
    T, D = x2.shape
    tm = ROW_TILE
    nt = T // tm
    per_seq = seq // tm
    nblk = tm // MOBA_BLOCK
    G = GROUP
    cw, cb, lg, lb, wp, pb, ps = local
    row = lambda i: (i, 0)
    tab = pl.BlockSpec((tm, G), lambda i: (i % per_seq, 0))
    vec = _const_spec((1, G))
    f32o = jax.ShapeDtypeStruct((T, G), F32)
    bfo = jax.ShapeDtypeStruct((T, G), BF16)
    blk3 = lambda rows, dt: jax.ShapeDtypeStruct((T // MOBA_BLOCK, rows, MOBA_BLOCK), dt)
    o_row = pl.BlockSpec((tm, G), row)
    o_blk = lambda rows: pl.BlockSpec((nblk, rows, MOBA_BLOCK), lambda i: (i, 0, 0))
    return pl.pallas_call(
        functools.partial(_in_proj_kernel, per_seq),
        grid=(nt,),
        in_specs=[pl.BlockSpec((tm, D), row), _const_spec((1, D)), _const_spec(w_in.shape),
                  tab, tab, tab, tab,
                  _const_spec(cw.shape), vec, vec, vec, _const_spec(wp.shape), vec, vec],
        out_specs=[o_row, o_row, o_row, o_row, o_row, o_blk(G), o_row, o_blk(N_HEADS * VT_ROWS),
                   pl.BlockSpec((1, nblk, G), lambda i: (i, 0, 0)), o_row],
        out_shape=[bfo, bfo, bfo, bfo, f32o, blk3(G, F32), bfo, blk3(N_HEADS * VT_ROWS, BF16),
                   jax.ShapeDtypeStruct((nt, nblk, G), F32), bfo],
        scratch_shapes=[pltpu.VMEM((CONV_HALO + tm, G), F32), pltpu.VMEM((POOL_HALO + tm, G), F32),
                        pltpu.VMEM((tm, G), BF16)],
        compiler_params=_cparams(("arbitrary",)),
        name="in_proj",
    )(x2, g, w_in, *tabs, cw, cb, lg, lb, wp, pb, ps)


def _split_dot(x, p):
    hi = x.astype(BF16)
    lo = (x - hi.astype(F32)).astype(BF16)
    return (jnp.dot(hi, p, preferred_element_type=F32) + jnp.dot(lo, p, preferred_element_type=F32))


def _retention_kernel(q_ref, k_ref, v_ref, g_ref, dm_ref, qd_ref, kd_ref, cd_ref, bd_ref, gg_ref, gb_ref,
                      y_ref, state):
    @pl.when(pl.program_id(1) == 0)
    def _():
        state[...] = jnp.zeros_like(state)

    seqs = range(q_ref.shape[0])
    head = _lane_head((1, GROUP), 1)
    nt = (((1,), (1,)), ((), ()))
    masks = [head == h for h in range(N_HEADS)]
    bd = bd_ref[...]
    p = (bd * (1.0 / HEAD_DIM)).astype(BF16)
    scores, o = [], []
    for b in seqs:
        q, k, v = q_ref[b], k_ref[b], v_ref[b]
        scores.append([lax.dot_general(jnp.where(mh, q, jnp.zeros_like(q)), k, nt, preferred_element_type=F32)
                       for mh in masks])
        o.append(jnp.dot(q, state[b].astype(BF16), preferred_element_type=F32) * qd_ref[...])
        kdec = (k.astype(F32) * kd_ref[...]).astype(BF16)
        upd = lax.dot_general(kdec, v, (((0,), (0,)), ((), ())), preferred_element_type=F32)
        state[b] = state[b] * cd_ref[...] + bd * upd
    for b in seqs:
        v = v_ref[b]
        for h, mh in enumerate(masks):
            oh = jnp.dot((scores[b][h] * dm_ref[h]).astype(BF16), v, preferred_element_type=F32)
            o[b] = o[b] + jnp.where(mh, oh, 0.0)
    d = [o[b] - _split_dot(o[b], p) for b in seqs]
    var = [_split_dot(d[b] * d[b], p) for b in seqs]
    for b in seqs:
        on = d[b] * lax.rsqrt(var[b] + 1e-5) * gg_ref[...] + gb_ref[...]
        y_ref[b] = (_silu(g_ref[b]) * on).astype(BF16)


def _retention(rq, rk, rv, rg, rtabs, gg, gb, batch, seq):
    T, G = rq.shape
    C = RET_CHUNK
    nc = seq // C
    ns = math.gcd(batch, RET_SEQS)
    row = pl.BlockSpec((ns, C, G), lambda b, c: (b, c, 0))
    dm, qd, kd, cd, bd = rtabs
    vec = _const_spec((1, G))
    per_seq = lambda t: t.reshape(batch, seq, G)
    y = pl.pallas_call(
        _retention_kernel,
        grid=(batch // ns, nc),
        in_specs=[row, row, row, row, _const_spec(dm.shape), _const_spec(qd.shape), _const_spec(kd.shape),
                  vec, _const_spec(bd.shape), vec, vec],
        out_specs=row,
        out_shape=jax.ShapeDtypeStruct((batch, seq, G), BF16),
        scratch_shapes=[pltpu.VMEM((ns, G, G), F32)],
        compiler_params=_cparams(("arbitrary", "arbitrary")),
        name="retention",
    )(per_seq(rq), per_seq(rk), per_seq(rv), per_seq(rg), dm, qd, kd, cd, bd, gg, gb)
    return y.reshape(T, G)


def _moba_kernel(qt_ref, k_ref, vt_ref, km_ref, y_ref, sel_ref, qh_ref, m_ref, acc_ref, ot_ref, sa_ref, sb_ref):
    a = pl.program_id(1)
    nb = km_ref.shape[1]
    Q = MOBA_BLOCK
    chains = [(t, h) for t in range(MOBA_TILES) for h in range(N_HEADS)]
    cid = lambda t, h: t * N_HEADS + h
    ch_head = _lane_head((GROUP, 1), 0)
    km = km_ref[0]
    km_head = _lane_head((1, GROUP), 1)
    blk = lax.broadcasted_iota(jnp.int32, (nb, Q), 0)

    km_heads = jnp.concatenate([jnp.where(km_head == h, km, 0.0) for h in range(N_HEADS)], axis=0)
    for t in range(MOBA_TILES):
        qt = qt_ref[t]
        qs = (qt * (HEAD_DIM ** -0.5 * LOG2E)).astype(BF16)
        gate_all = jnp.dot(km_heads, qt, preferred_element_type=F32, precision=lax.Precision.HIGHEST)
        past = blk < MOBA_TILES * a + t
        for h in range(N_HEADS):
            gate = jnp.where(past, gate_all[h * nb:(h + 1) * nb], -jnp.inf)
            sel = jnp.zeros((nb, Q), F32)
            for _ in range(MOBA_TOPK):
                top = jnp.max(gate, axis=0, keepdims=True)
                first = jnp.min(jnp.where(gate == top, blk, nb), axis=0, keepdims=True)
                pick = (blk == first) & past
                sel = jnp.where(pick, 1.0, sel)
                gate = jnp.where(pick, -jnp.inf, gate)
            sel_ref[cid(t, h) * nb:(cid(t, h) + 1) * nb, :] = sel
            qh_ref[cid(t, h)] = jnp.where(ch_head == h, qs, jnp.zeros_like(qs))

    m_ref[...] = jnp.full(m_ref.shape, -BIG, F32)
    acc_ref[...] = jnp.zeros(acc_ref.shape, F32)

    def score(j, s_ref, which):
        kb = k_ref[j]
        for t, h in which:
            s_ref[cid(t, h)] = jnp.dot(kb, qh_ref[cid(t, h)], preferred_element_type=F32)

    def absorb(j, s_ref, which, own_tile=None):
        causal = None
        for t, h in which:
            c = cid(t, h)
            m = m_ref[c:c + 1, :]
            s = s_ref[c]
            if t == own_tile:
                if causal is None:
                    causal = (lax.broadcasted_iota(jnp.int32, (Q, Q), 0)
                              <= lax.broadcasted_iota(jnp.int32, (Q, Q), 1))
                s = jnp.where(causal, s, NEG)
                m_new = jnp.maximum(m, jnp.max(s, axis=0, keepdims=True))
                shift = m_new
            else:
                on = sel_ref[pl.ds(c * nb + j, 1), :] > 0.0
                m_new = jnp.where(on, jnp.maximum(m, jnp.max(s, axis=0, keepdims=True)), m)
                shift = jnp.where(on, m_new, BIG)
            p = jnp.exp2(s - shift).astype(BF16)
            vt = vt_ref[j, h * VT_ROWS:(h + 1) * VT_ROWS, :]
            acc_ref[c] = jnp.exp2(m - m_new) * acc_ref[c] + jnp.dot(vt, p, preferred_element_type=F32)
            m_ref[c:c + 1, :] = m_new

    def body(i, carry):
        score(2 * i + 1, sb_ref, chains)
        absorb(2 * i, sa_ref, chains)
        score(2 * i + 2, sa_ref, chains)
        absorb(2 * i + 1, sb_ref, chains)
        return carry

    first = MOBA_TILES * a
    second = [(1, h) for h in range(N_HEADS)]
    score(0, sa_ref, chains)
    lax.fori_loop(0, a, body, 0)
    score(first + 1, sb_ref, second)
    absorb(first, sa_ref, chains, own_tile=0)
    absorb(first + 1, sb_ref, second, own_tile=1)

    for t in range(MOBA_TILES):
        for h in range(N_HEADS):
            acc = acc_ref[cid(t, h)]
            ot_ref[h * HEAD_DIM:(h + 1) * HEAD_DIM, :] = acc[:HEAD_DIM] / acc[HEAD_DIM:HEAD_DIM + 1]
        y_ref[t * Q:(t + 1) * Q, :] = ot_ref[...].T.astype(BF16)


def _moba(mqt, mk3, mvt, km, batch, seq):
    nb = seq // MOBA_BLOCK
    B = MOBA_BLOCK
    G = GROUP
    nt = MOBA_TILES
    assert nt == 2 and nb % nt == 0
    steps = nb // nt
    nc = nt * N_HEADS
    return pl.pallas_call(
        _moba_kernel,
        grid=(batch, steps),
        in_specs=[pl.BlockSpec((nt, G, B), lambda b, i: (b * steps + i, 0, 0)),
                  pl.BlockSpec((nb, B, G), lambda b, i: (b, 0, 0)),
                  pl.BlockSpec((nb, N_HEADS * VT_ROWS, B), lambda b, i: (b, 0, 0)),
                  pl.BlockSpec((1, nb, G), lambda b, i: (b, 0, 0))],
        out_specs=pl.BlockSpec((nt * B, G), lambda b, i: (b * steps + i, 0)),
        out_shape=jax.ShapeDtypeStruct((batch * seq, G), BF16),
        scratch_shapes=[pltpu.VMEM((nc * nb, B), F32),
                        pltpu.VMEM((nc, G, B), BF16),
                        pltpu.VMEM((nc, B), F32),
                        pltpu.VMEM((nc, VT_ROWS, B), F32),
                        pltpu.VMEM((G, B), F32),
                        pltpu.VMEM((nc, B, B), F32),
                        pltpu.VMEM((nc, B, B), F32)],
        compiler_params=_cparams(("arbitrary", "arbitrary")),
        name="moba",
    )(mqt, mk3, mvt, km)


def _post_kernel(x_ref, ya_ref, yb_ref, yc_ref, yd_ref, wo_ref, gpo_ref, gfi_ref,
                 wg_ref, wu_ref, wd_ref, gfo_ref, o_ref, f_scr):
    G = GROUP
    y = jnp.dot(ya_ref[...], wo_ref[0:G, :], preferred_element_type=F32)
    y = y + jnp.dot(yb_ref[...], wo_ref[G:2 * G, :], preferred_element_type=F32)
    y = y + jnp.dot(yc_ref[...], wo_ref[2 * G:3 * G, :], preferred_element_type=F32)
    y = y + jnp.dot(yd_ref[...], wo_ref[3 * G:4 * G, :], preferred_element_type=F32)
    x1 = x_ref[...] + _rms(y, gpo_ref[...])
    hb = _rms(x1, gfi_ref[...]).astype(BF16)
    d_ff = wg_ref.shape[1]
    for c in range(0, d_ff, FF_CHUNK):
        cols = slice(c, c + FF_CHUNK)
        gt = jnp.dot(hb, wg_ref[:, cols], preferred_element_type=F32)
        up = jnp.dot(hb, wu_ref[:, cols], preferred_element_type=F32)
        f_scr[:, cols] = (_silu(gt) * up).astype(BF16)
    f = jnp.dot(f_scr[...], wd_ref[...], preferred_element_type=F32)
    o_ref[...] = x1 + _rms(f, gfo_ref[...])


def _post(x2, ya, yb, yc, yd, wo, gpo, gfi, wg, wu, wd, gfo):
    T, D = x2.shape
    tm = ROW_TILE
    G = GROUP
    d_ff = wg.shape[1]
    row = lambda i: (i, 0)
    xs = pl.BlockSpec((tm, D), row)
    ys = pl.BlockSpec((tm, G), row)
    vec = _const_spec((1, D))
    return pl.pallas_call(
        _post_kernel,
        grid=(T // tm,),
        in_specs=[xs, ys, ys, ys, ys, _const_spec(wo.shape), vec, vec,
                  _const_spec(wg.shape), _const_spec(wu.shape), _const_spec(wd.shape), vec],
        out_specs=xs,
        out_shape=jax.ShapeDtypeStruct((T, D), F32),
        scratch_shapes=[pltpu.VMEM((tm, d_ff), BF16)],
        compiler_params=_cparams(("arbitrary",)),
        name="out_proj_ffn",
    )(x2, ya, yb, yc, yd, wo, gpo, gfi, wg, wu, wd, gfo)


def _rotary_tables(seq, rot_dim, theta):
    pos = jnp.arange(seq, dtype=F32)
    inv = 1.0 / (theta ** (jnp.arange(0, rot_dim, 2, dtype=F32) / rot_dim))
    ang = pos[:, None] * inv[None, :]
    cos, sin = jnp.cos(ang), jnp.sin(ang)
    rest = HEAD_DIM - rot_dim
    cos_h = jnp.concatenate([cos, cos, jnp.ones((seq, rest), F32)], axis=-1)
    sin_h = jnp.concatenate([-sin, sin, jnp.zeros((seq, rest), F32)], axis=-1)
    return jnp.tile(cos_h, (1, N_HEADS)), jnp.tile(sin_h, (1, N_HEADS))


def _retention_tables():
    c = RET_CHUNK
    log_gamma = jnp.log(1.0 - 2.0 ** (-5.0 - jnp.arange(N_HEADS, dtype=F32)))
    idx = jnp.arange(c, dtype=F32)
    diff = idx[:, None] - idx[None, :]
    dm = jnp.where(diff[None] >= 0, jnp.exp(jnp.maximum(diff, 0.0)[None] * log_gamma[:, None, None]), 0.0)
    lanes = lambda t: jnp.repeat(t, HEAD_DIM, axis=-1)
    qd = lanes(jnp.exp((idx + 1.0)[:, None] * log_gamma[None, :]))
    kd = lanes(jnp.exp((c - 1.0 - idx)[:, None] * log_gamma[None, :]))
    cd = lanes(jnp.exp(c * log_gamma)[None, :])
    head = jnp.arange(GROUP) // HEAD_DIM
    bd = (head[:, None] == head[None, :]).astype(F32)
    return dm, qd, kd, cd, bd


def _block_diag(w):
    n, c, _ = w.shape
    eye = jnp.eye(n, dtype=w.dtype)
    return (eye[:, None, :, None] * w[:, :, None, :]).reshape(n * c, n * c)


def kernel(x, attn_pre_g, w_in, conv_w, conv_b, conv_ln_g, conv_ln_b, ret_gn_g, ret_gn_b, pool_w, pool_b,
           pool_scale, w_out, attn_post_g, ffn_pre_g, w_gate, w_up, w_down, ffn_post_g):
    batch, seq, d_model = x.shape
    depth = w_in.shape[0]
    assert seq % ROW_TILE == 0 and ROW_TILE % LOCAL_CHUNK == 0 and seq % RET_CHUNK == 0 and seq % MOBA_BLOCK == 0
    nb = seq // MOBA_BLOCK
    tabs = _rotary_tables(seq, HEAD_DIM, RET_ROT_THETA) + _rotary_tables(seq, ROT_DIM, ROPE_THETA)
    rtabs = _retention_tables()
    vec = lambda a: a.reshape(1, -1)

    x2 = x.reshape(batch * seq, d_model)
    for l in range(depth):
        local = (conv_w[l], vec(conv_b[l]), vec(conv_ln_g[l]), vec(conv_ln_b[l]),
                 _block_diag(pool_w[l]).astype(BF16), vec(pool_b[l]), vec(pool_scale[l]))
        ya, rq, rk, rv, rg, mqt, mk, mvt, km, yd = _in_proj(
            x2, vec(attn_pre_g[l]), w_in[l].astype(BF16), tabs, local, seq)
        yb = _retention(rq, rk, rv, rg, rtabs, vec(ret_gn_g[l]), vec(ret_gn_b[l]), batch, seq)
        yc = _moba(mqt, mk.reshape(batch * nb, MOBA_BLOCK, GROUP), mvt, km.reshape(batch, nb, GROUP), batch, seq)
        x2 = _post(x2, ya, yb, yc, yd, w_out[l].astype(BF16), vec(attn_post_g[l]), vec(ffn_pre_g[l]),
                   w_gate[l].astype(BF16), w_up[l].astype(BF16), w_down[l].astype(BF16), vec(ffn_post_g[l]))
    return x2.reshape(batch, seq, d_model)
```

```python
import functools
import math

import jax
import jax.numpy as jnp
from jax import lax
from jax.experimental import pallas as pl
from jax.experimental.pallas import tpu as pltpu

F32 = jnp.float32
BF16 = jnp.bfloat16

GROUP = 256
HEAD_DIM = 64
N_HEADS = GROUP // HEAD_DIM
CONV_WIDTH = 31
RET_ROT_THETA = 10000.0
MOBA_BLOCK = 256
MOBA_TOPK = 3
ROPE_THETA = 500000.0
ROT_DIM = HEAD_DIM // 4
POOL_WINDOWS = (2, 4, 8, 16)
NEG = -1e30
BIG = 1e30
VT_ROWS = HEAD_DIM + 16
LOG2E = math.log2(math.e)

ROW_TILE = 1024
POST_TILE = 1024
LOCAL_CHUNK = 64
CONV_HALO = 32
POOL_HALO = 16
RET_CHUNK = 256
FF_CHUNK = 256
MOBA_TILES = 2
VMEM_LIMIT = 56 * 1024 * 1024


def _cparams(sem):
    return pltpu.CompilerParams(dimension_semantics=sem, vmem_limit_bytes=VMEM_LIMIT)


def _const_spec(shape):
    nd = len(shape)
    return pl.BlockSpec(shape, lambda *_: (0,) * nd, pipeline_mode=pl.Buffered(1))


def _rms(x, g):
    return x * lax.rsqrt(jnp.mean(x * x, axis=-1, keepdims=True) + 1e-6) * g


def _silu(x):
    return x * jax.nn.sigmoid(x)


def _lane_head(shape, axis):
    return lax.broadcasted_iota(jnp.int32, shape, axis) // HEAD_DIM


def _rotate_half(x, cos, sin_signed, half):
    n = x.shape[-1]
    lane = lax.broadcasted_iota(jnp.int32, x.shape, 1) % HEAD_DIM
    partner = jnp.where(lane < half, pltpu.roll(x, n - half, 1), pltpu.roll(x, half, 1))
    return x * cos + partner * sin_signed


def _shift_halo(buf, halo, ts, first):
    @pl.when(first)
    def _():
        buf[0:halo, :] = jnp.zeros((halo, GROUP), F32)

    @pl.when(jnp.logical_not(first))
    def _():
        buf[0:halo, :] = buf[ts:ts + halo, :]


def _conv_chunk(hbuf, r0, rc, cw_ref, cb_ref, lg_ref, lb_ref):
    back = 8 * ((CONV_WIDTH - 1) // 8)
    acc = jnp.broadcast_to(cb_ref[...], (rc, GROUP))
    for r in range(8):
        lo = CONV_HALO + r0 - back - r
        hr = hbuf[lo:lo + back + rc, :]
        part = None
        for q in range(back // 8 + 1):
            d = 8 * q + r
            if d < CONV_WIDTH:
                j = CONV_WIDTH - 1 - d
                term = cw_ref[j:j + 1, :] * hr[back - 8 * q:back - 8 * q + rc]
                part = term if part is None else part + term
        acc = acc + part
    mu = jnp.mean(acc, axis=-1, keepdims=True)
    d = acc - mu
    var = jnp.mean(d * d, axis=-1, keepdims=True)
    return _silu(d * lax.rsqrt(var + 1e-5) * lg_ref[...] + lb_ref[...])


def _pool_chunk(ubuf, r0, rc, t0):
    lane_grp = lax.broadcasted_iota(jnp.int32, (rc, GROUP), 1) // (GROUP // len(POOL_WINDOWS))
    t1 = (t0 + r0 + 1 + lax.broadcasted_iota(jnp.int32, (rc, 1), 0)).astype(F32)
    run = ubuf[POOL_HALO + r0 - 8:POOL_HALO + r0 + rc, :]
    u0 = run[8:]
    totals = {}
    for dlt in range(1, 8):
        run = run + ubuf[POOL_HALO + r0 - 8 - dlt:POOL_HALO + r0 + rc - dlt, :]
        totals[dlt + 1] = run[8:]
    totals[16] = run[8:] + run[:rc]
    pooled = jnp.zeros((rc, GROUP), F32)
    for gi, w in enumerate(POOL_WINDOWS):
        pooled = jnp.where(lane_grp == gi, totals[w] / jnp.minimum(t1, float(w)) - u0, pooled)
    return pooled


def _split_dot(x, p):
    hi = x.astype(BF16)
    lo = (x - hi.astype(F32)).astype(BF16)
    return (jnp.dot(hi, p, preferred_element_type=F32) + jnp.dot(lo, p, preferred_element_type=F32))


def _retention_tile(q, k, v, g, state, dm_ref, qd_ref, kd_ref, cd_ref, bd_ref, gg_ref, gb_ref, y_ref):
    C = RET_CHUNK
    chunks = [slice(r, r + C) for r in range(0, q.shape[0], C)]
    head = _lane_head((1, GROUP), 1)
    nt = (((1,), (1,)), ((), ()))
    masks = [head == h for h in range(N_HEADS)]
    bd = bd_ref[...]
    p = (bd * (1.0 / HEAD_DIM)).astype(BF16)
    scores = [[lax.dot_general(jnp.where(mh, q[rows], jnp.zeros_like(q[rows])), k[rows], nt,
                               preferred_element_type=F32) for mh in masks] for rows in chunks]
    o = []
    for rows in chunks:
        o.append(jnp.dot(q[rows], state[...].astype(BF16), preferred_element_type=F32) * qd_ref[...])
        kdec = (k[rows].astype(F32) * kd_ref[...]).astype(BF16)
        upd = lax.dot_general(kdec, v[rows], (((0,), (0,)), ((), ())), preferred_element_type=F32)
        state[...] = state[...] * cd_ref[...] + bd * upd
    for c, rows in enumerate(chunks):
        for h, mh in enumerate(masks):
            oh = jnp.dot((scores[c][h] * dm_ref[h]).astype(BF16), v[rows], preferred_element_type=F32)
            o[c] = o[c] + jnp.where(mh, oh, 0.0)
    d = [oc - _split_dot(oc, p) for oc in o]
    var = [_split_dot(dc * dc, p) for dc in d]
    for c, rows in enumerate(chunks):
        on = d[c] * lax.rsqrt(var[c] + 1e-5) * gg_ref[...] + gb_ref[...]
        y_ref[rows, :] = (_silu(g[rows]) * on).astype(BF16)


def _in_proj_kernel(per_seq, x_ref, g_ref, w_ref, cr_ref, sr_ref, cm_ref, sm_ref,
                    cw_ref, cb_ref, lg_ref, lb_ref, wp_ref, pb_ref, ps_ref,
                    dm_ref, qd_ref, kd_ref, cd_ref, bd_ref, gg_ref, gb_ref,
                    ya_ref, yb_ref, mqt_ref, mk_ref, mvt_ref, km_ref, yd_ref,
                    hbuf, ubuf, pool_scr, state):
    G = GROUP
    ts = x_ref.shape[0]
    s = pl.program_id(0) % per_seq
    _shift_halo(hbuf, CONV_HALO, ts, s == 0)
    _shift_halo(ubuf, POOL_HALO, ts, s == 0)

    @pl.when(s == 0)
    def _():
        state[...] = jnp.zeros_like(state)

    hb = _rms(x_ref[...], g_ref[...]).astype(BF16)

    def proj(i, n=1):
        return jnp.dot(hb, w_ref[:, i * G:(i + n) * G], preferred_element_type=F32)

    uc = proj(0, 2)
    hbuf[CONV_HALO:, :] = uc[:, :G] * jax.nn.sigmoid(uc[:, G:])
    ubuf[POOL_HALO:, :] = proj(9)
    cr, sr = cr_ref[...], sr_ref[...]
    rq = _rotate_half(proj(2), cr, sr, HEAD_DIM // 2).astype(BF16)
    rk = (_rotate_half(proj(3), cr, sr, HEAD_DIM // 2) * (HEAD_DIM ** -0.5)).astype(BF16)
    rv = proj(4).astype(BF16)
    rg = proj(5)
    cm, sm = cm_ref[...], sm_ref[...]
    mq = _rotate_half(proj(6), cm, sm, ROT_DIM // 2)
    mk = _rotate_half(proj(7), cm, sm, ROT_DIM // 2)
    mv = proj(8)
    mk_ref[...] = mk.astype(BF16)
    nblk = mq.shape[0] // MOBA_BLOCK
    for j in range(nblk):
        rows = slice(j * MOBA_BLOCK, (j + 1) * MOBA_BLOCK)
        mqt_ref[j] = mq[rows].T
        vt = mv[rows].T.astype(BF16)
        for h in range(N_HEADS):
            r = h * VT_ROWS
            mvt_ref[j, r:r + HEAD_DIM, :] = vt[h * HEAD_DIM:(h + 1) * HEAD_DIM]
            mvt_ref[j, r + HEAD_DIM:r + VT_ROWS, :] = jnp.ones((VT_ROWS - HEAD_DIM, MOBA_BLOCK), BF16)
        km_ref[0, j:j + 1, :] = jnp.mean(mk[rows], axis=0, keepdims=True)

    _retention_tile(rq, rk, rv, rg, state, dm_ref, qd_ref, kd_ref, cd_ref, bd_ref, gg_ref, gb_ref, yb_ref)

    rc = LOCAL_CHUNK
    for r0 in range(0, ts, rc):
        ya_ref[r0:r0 + rc, :] = _conv_chunk(hbuf, r0, rc, cw_ref, cb_ref, lg_ref, lb_ref).astype(BF16)
        pool_scr[r0:r0 + rc, :] = _pool_chunk(ubuf, r0, rc, s * ts).astype(BF16)
    y = jnp.dot(pool_scr[...], wp_ref[...], preferred_element_type=F32)
    yd_ref[...] = ((y + pb_ref[...]) * ps_ref[...]).astype(BF16)


def _in_proj(x2, g, w_in, tabs, local, ret, seq):
    T, D = x2.shape
    tm = ROW_TILE
    nt = T // tm
    per_seq = seq // tm
    nblk = tm // MOBA_BLOCK
    G = GROUP
    cw, cb, lg, lb, wp, pb, ps = local
    dm, qd, kd, cd, bd, gg, gb = ret
    row = lambda i: (i, 0)
    tab = pl.BlockSpec((tm, G), lambda i: (i % per_seq, 0))
    vec = _const_spec((1, G))
    bfo = jax.ShapeDtypeStruct((T, G), BF16)
    blk3 = lambda rows, dt: jax.ShapeDtypeStruct((T // MOBA_BLOCK, rows, MOBA_BLOCK), dt)
    o_row = pl.BlockSpec((tm, G), row)
    o_blk = lambda rows: pl.BlockSpec((nblk, rows, MOBA_BLOCK), lambda i: (i, 0, 0))
    return pl.pallas_call(
        functools.partial(_in_proj_kernel, per_seq),
        grid=(nt,),
        in_specs=[pl.BlockSpec((tm, D), row), _const_spec((1, D)), _const_spec(w_in.shape),
                  tab, tab, tab, tab,
                  _const_spec(cw.shape), vec, vec, vec, _const_spec(wp.shape), vec, vec,
                  _const_spec(dm.shape), _const_spec(qd.shape), _const_spec(kd.shape), vec, _const_spec(bd.shape),
                  vec, vec],
        out_specs=[o_row, o_row, o_blk(G), o_row, o_blk(N_HEADS * VT_ROWS),
                   pl.BlockSpec((1, nblk, G), lambda i: (i, 0, 0)), o_row],
        out_shape=[bfo, bfo, blk3(G, F32), bfo, blk3(N_HEADS * VT_ROWS, BF16),
                   jax.ShapeDtypeStruct((nt, nblk, G), F32), bfo],
        scratch_shapes=[pltpu.VMEM((CONV_HALO + tm, G), F32), pltpu.VMEM((POOL_HALO + tm, G), F32),
                        pltpu.VMEM((tm, G), BF16), pltpu.VMEM((G, G), F32)],
        compiler_params=_cparams(("arbitrary",)),
        name="in_proj",
    )(x2, g, w_in, *tabs, cw, cb, lg, lb, wp, pb, ps, dm, qd, kd, cd, bd, gg, gb)


def _moba_kernel(qt_ref, k_ref, vt_ref, km_ref, y_ref, sel_ref, qh_ref, m_ref, acc_ref, ot_ref, sa_ref, sb_ref):
    a = pl.program_id(1)
    nb = km_ref.shape[1]
    Q = MOBA_BLOCK
    chains = [(t, h) for t in range(MOBA_TILES) for h in range(N_HEADS)]
    cid = lambda t, h: t * N_HEADS + h
    ch_head = _lane_head((GROUP, 1), 0)
    km = km_ref[0]
    km_head = _lane_head((1, GROUP), 1)
    blk = lax.broadcasted_iota(jnp.int32, (nb, Q), 0)

    km_heads = jnp.concatenate([jnp.where(km_head == h, km, 0.0) for h in range(N_HEADS)], axis=0)
    for t in range(MOBA_TILES):
        qt = qt_ref[t]
        qs = (qt * (HEAD_DIM ** -0.5 * LOG2E)).astype(BF16)
        gate_all = jnp.dot(km_heads, qt, preferred_element_type=F32, precision=lax.Precision.HIGHEST)
        past = blk < MOBA_TILES * a + t
        for h in range(N_HEADS):
            gate = jnp.where(past, gate_all[h * nb:(h + 1) * nb], -jnp.inf)
            sel = jnp.zeros((nb, Q), F32)
            for _ in range(MOBA_TOPK):
                top = jnp.max(gate, axis=0, keepdims=True)
                first = jnp.min(jnp.where(gate == top, blk, nb), axis=0, keepdims=True)
                pick = (blk == first) & past
                sel = jnp.where(pick, 1.0, sel)
                gate = jnp.where(pick, -jnp.inf, gate)
            sel_ref[cid(t, h) * nb:(cid(t, h) + 1) * nb, :] = sel
            qh_ref[cid(t, h)] = jnp.where(ch_head == h, qs, jnp.zeros_like(qs))

    m_ref[...] = jnp.full(m_ref.shape, -BIG, F32)
    acc_ref[...] = jnp.zeros(acc_ref.shape, F32)

    def score(j, s_ref, which):
        kb = k_ref[j]
        for t, h in which:
            s_ref[cid(t, h)] = jnp.dot(kb, qh_ref[cid(t, h)], preferred_element_type=F32)

    def absorb(j, s_ref, which, own_tile=None):
        causal = None
        for t, h in which:
            c = cid(t, h)
            m = m_ref[c:c + 1, :]
            s = s_ref[c]
            if t == own_tile:
                if causal is None:
                    causal = (lax.broadcasted_iota(jnp.int32, (Q, Q), 0)
                              <= lax.broadcasted_iota(jnp.int32, (Q, Q), 1))
                s = jnp.where(causal, s, NEG)
                m_new = jnp.maximum(m, jnp.max(s, axis=0, keepdims=True))
                shift = m_new
            else:
                on = sel_ref[pl.ds(c * nb + j, 1), :] > 0.0
                m_new = jnp.where(on, jnp.maximum(m, jnp.max(s, axis=0, keepdims=True)), m)
                shift = jnp.where(on, m_new, BIG)
            p = jnp.exp2(s - shift).astype(BF16)
            vt = vt_ref[j, h * VT_ROWS:(h + 1) * VT_ROWS, :]
            acc_ref[c] = jnp.exp2(m - m_new) * acc_ref[c] + jnp.dot(vt, p, preferred_element_type=F32)
            m_ref[c:c + 1, :] = m_new

    def body(i, carry):
        score(2 * i + 1, sb_ref, chains)
        absorb(2 * i, sa_ref, chains)
        score(2 * i + 2, sa_ref, chains)
        absorb(2 * i + 1, sb_ref, chains)
        return carry

    first = MOBA_TILES * a
    score(0, sa_ref, chains)
    lax.fori_loop(0, first // 2, body, 0)
    bufs = (sa_ref, sb_ref)
    for k in range(MOBA_TILES):
        if k + 1 < MOBA_TILES:
            score(first + k + 1, bufs[(k + 1) % 2], [c for c in chains if c[0] > k])
        absorb(first + k, bufs[k % 2], [c for c in chains if c[0] >= k], own_tile=k)

    for t in range(MOBA_TILES):
        for h in range(N_HEADS):
            acc = acc_ref[cid(t, h)]
            ot_ref[h * HEAD_DIM:(h + 1) * HEAD_DIM, :] = acc[:HEAD_DIM] / acc[HEAD_DIM:HEAD_DIM + 1]
        y_ref[t * Q:(t + 1) * Q, :] = ot_ref[...].T.astype(BF16)


def _moba(mqt, mk3, mvt, km, batch, seq):
    nb = seq // MOBA_BLOCK
    B = MOBA_BLOCK
    G = GROUP
    nt = MOBA_TILES
    assert nt % 2 == 0 and nb % nt == 0
    steps = nb // nt
    nc = nt * N_HEADS
    return pl.pallas_call(
        _moba_kernel,
        grid=(batch, steps),
        in_specs=[pl.BlockSpec((nt, G, B), lambda b, i: (b * steps + i, 0, 0)),
                  pl.BlockSpec((nb, B, G), lambda b, i: (b, 0, 0)),
                  pl.BlockSpec((nb, N_HEADS * VT_ROWS, B), lambda b, i: (b, 0, 0)),
                  pl.BlockSpec((1, nb, G), lambda b, i: (b, 0, 0))],
        out_specs=pl.BlockSpec((nt * B, G), lambda b, i: (b * steps + i, 0)),
        out_shape=jax.ShapeDtypeStruct((batch * seq, G), BF16),
        scratch_shapes=[pltpu.VMEM((nc * nb, B), F32),
                        pltpu.VMEM((nc, G, B), BF16),
                        pltpu.VMEM((nc, B), F32),
                        pltpu.VMEM((nc, VT_ROWS, B), F32),
                        pltpu.VMEM((G, B), F32),
                        pltpu.VMEM((nc, B, B), F32),
                        pltpu.VMEM((nc, B, B), F32)],
        compiler_params=_cparams(("arbitrary", "arbitrary")),
        name="moba",
    )(mqt, mk3, mvt, km)


def _post_kernel(x_ref, ya_ref, yb_ref, yc_ref, yd_ref, wo_ref, gpo_ref, gfi_ref,
                 wg_ref, wu_ref, wd_ref, gfo_ref, o_ref, f_scr):
    G = GROUP
    y = jnp.dot(ya_ref[...], wo_ref[0:G, :], preferred_element_type=F32)
    y = y + jnp.dot(yb_ref[...], wo_ref[G:2 * G, :], preferred_element_type=F32)
    y = y + jnp.dot(yc_ref[...], wo_ref[2 * G:3 * G, :], preferred_element_type=F32)
    y = y + jnp.dot(yd_ref[...], wo_ref[3 * G:4 * G, :], preferred_element_type=F32)
    x1 = x_ref[...] + _rms(y, gpo_ref[...])
    hb = _rms(x1, gfi_ref[...]).astype(BF16)
    d_ff = wg_ref.shape[1]
    for c in range(0, d_ff, FF_CHUNK):
        cols = slice(c, c + FF_CHUNK)
        gt = jnp.dot(hb, wg_ref[:, cols], preferred_element_type=F32)
        up = jnp.dot(hb, wu_ref[:, cols], preferred_element_type=F32)
        f_scr[:, cols] = (_silu(gt) * up).astype(BF16)
    f = jnp.dot(f_scr[...], wd_ref[...], preferred_element_type=F32)
    o_ref[...] = x1 + _rms(f, gfo_ref[...])


def _post(x2, ya, yb, yc, yd, wo, gpo, gfi, wg, wu, wd, gfo):
    T, D = x2.shape
    tm = POST_TILE
    G = GROUP
    d_ff = wg.shape[1]
    row = lambda i: (i, 0)
    xs = pl.BlockSpec((tm, D), row)
    ys = pl.BlockSpec((tm, G), row)
    vec = _const_spec((1, D))
    return pl.pallas_call(
        _post_kernel,
        grid=(T // tm,),
        in_specs=[xs, ys, ys, ys, ys, _const_spec(wo.shape), vec, vec,
                  _const_spec(wg.shape), _const_spec(wu.shape), _const_spec(wd.shape), vec],
        out_specs=xs,
        out_shape=jax.ShapeDtypeStruct((T, D), F32),
        scratch_shapes=[pltpu.VMEM((tm, d_ff), BF16)],
        compiler_params=_cparams(("arbitrary",)),
        name="out_proj_ffn",
    )(x2, ya, yb, yc, yd, wo, gpo, gfi, wg, wu, wd, gfo)


def _rotary_tables(seq, rot_dim, theta):
    pos = jnp.arange(seq, dtype=F32)
    inv = 1.0 / (theta ** (jnp.arange(0, rot_dim, 2, dtype=F32) / rot_dim))
    ang = pos[:, None] * inv[None, :]
    cos, sin = jnp.cos(ang), jnp.sin(ang)
    rest = HEAD_DIM - rot_dim
    cos_h = jnp.concatenate([cos, cos, jnp.ones((seq, rest), F32)], axis=-1)
    sin_h = jnp.concatenate([-sin, sin, jnp.zeros((seq, rest), F32)], axis=-1)
    return jnp.tile(cos_h, (1, N_HEADS)), jnp.tile(sin_h, (1, N_HEADS))


def _retention_tables():
    c = RET_CHUNK
    log_gamma = jnp.log(1.0 - 2.0 ** (-5.0 - jnp.arange(N_HEADS, dtype=F32)))
    idx = jnp.arange(c, dtype=F32)
    diff = idx[:, None] - idx[None, :]
    dm = jnp.where(diff[None] >= 0, jnp.exp(jnp.maximum(diff, 0.0)[None] * log_gamma[:, None, None]), 0.0)
    lanes = lambda t: jnp.repeat(t, HEAD_DIM, axis=-1)
    qd = lanes(jnp.exp((idx + 1.0)[:, None] * log_gamma[None, :]))
    kd = lanes(jnp.exp((c - 1.0 - idx)[:, None] * log_gamma[None, :]))
    cd = lanes(jnp.exp(c * log_gamma)[None, :])
    head = jnp.arange(GROUP) // HEAD_DIM
    bd = (head[:, None] == head[None, :]).astype(F32)
    return dm, qd, kd, cd, bd


def _block_diag(w):
    n, c, _ = w.shape
    eye = jnp.eye(n, dtype=w.dtype)
    return (eye[:, None, :, None] * w[:, :, None, :]).reshape(n * c, n * c)


def kernel(x, attn_pre_g, w_in, conv_w, conv_b, conv_ln_g, conv_ln_b, ret_gn_g, ret_gn_b, pool_w, pool_b,
           pool_scale, w_out, attn_post_g, ffn_pre_g, w_gate, w_up, w_down, ffn_post_g):
    batch, seq, d_model = x.shape
    depth = w_in.shape[0]
    assert seq % ROW_TILE == 0 and ROW_TILE % LOCAL_CHUNK == 0 and seq % RET_CHUNK == 0 and seq % MOBA_BLOCK == 0
    assert (batch * seq) % POST_TILE == 0
    nb = seq // MOBA_BLOCK
    tabs = _rotary_tables(seq, HEAD_DIM, RET_ROT_THETA) + _rotary_tables(seq, ROT_DIM, ROPE_THETA)
    rtabs = _retention_tables()
    vec = lambda a: a.reshape(1, -1)

    x2 = x.reshape(batch * seq, d_model)
    for l in range(depth):
        local = (conv_w[l], vec(conv_b[l]), vec(conv_ln_g[l]), vec(conv_ln_b[l]),
                 _block_diag(pool_w[l]).astype(BF16), vec(pool_b[l]), vec(pool_scale[l]))
        ret = rtabs + (vec(ret_gn_g[l]), vec(ret_gn_b[l]))
        ya, yb, mqt, mk, mvt, km, yd = _in_proj(
            x2, vec(attn_pre_g[l]), w_in[l].astype(BF16), tabs, local, ret, seq)
        yc = _moba(mqt, mk.reshape(batch * nb, MOBA_BLOCK, GROUP), mvt, km.reshape(batch, nb, GROUP), batch, seq)
        x2 = _post(x2, ya, yb, yc, yd, w_out[l].astype(BF16), vec(attn_post_g[l]), vec(ffn_pre_g[l]),
                   w_gate[l].astype(BF16), w_up[l].astype(BF16), w_down[l].astype(BF16), vec(ffn_post_g[l]))
    return x2.reshape(batch, seq, d_model)
```

```python
import functools
import math

import jax
import jax.numpy as jnp
from jax import lax
from jax.experimental import pallas as pl
from jax.experimental.pallas import tpu as pltpu

F32 = jnp.float32
BF16 = jnp.bfloat16

GROUP = 256
HEAD_DIM = 64
N_HEADS = GROUP // HEAD_DIM
CONV_WIDTH = 31
RET_ROT_THETA = 10000.0
MOBA_BLOCK = 256
MOBA_TOPK = 3
ROPE_THETA = 500000.0
ROT_DIM = HEAD_DIM // 4
POOL_WINDOWS = (2, 4, 8, 16)
NEG = -1e30
BIG = 1e30
VT_ROWS = HEAD_DIM + 16
LOG2E = math.log2(math.e)

ROW_TILE = 1024
POST_TILE = 1024
LOCAL_CHUNK = 64
CONV_HALO = 32
POOL_HALO = 16
RET_CHUNK = 256
FF_CHUNK = 256
MOBA_TILES = 2
VMEM_LIMIT = 56 * 1024 * 1024


def _cparams(sem):
    return pltpu.CompilerParams(dimension_semantics=sem, vmem_limit_bytes=VMEM_LIMIT)


def _const_spec(shape):
    nd = len(shape)
    return pl.BlockSpec(shape, lambda *_: (0,) * nd, pipeline_mode=pl.Buffered(1))


def _layer_spec(stacked, layer):
    shape = stacked.shape[1:]
    nd = len(shape)
    return pl.BlockSpec((None,) + shape, lambda *_: (layer,) + (0,) * nd, pipeline_mode=pl.Buffered(1))


def _rms(x, g):
    return x * lax.rsqrt(jnp.mean(x * x, axis=-1, keepdims=True) + 1e-6) * g


def _silu(x):
    return x * jax.nn.sigmoid(x)


def _lane_head(shape, axis):
    return lax.broadcasted_iota(jnp.int32, shape, axis) // HEAD_DIM


def _rotate_half(x, cos, sin_signed, half):
    n = x.shape[-1]
    lane = lax.broadcasted_iota(jnp.int32, x.shape, 1) % HEAD_DIM
    partner = jnp.where(lane < half, pltpu.roll(x, n - half, 1), pltpu.roll(x, half, 1))
    return x * cos + partner * sin_signed


def _shift_halo(buf, halo, ts, first):
    @pl.when(first)
    def _():
        buf[0:halo, :] = jnp.zeros((halo, GROUP), F32)

    @pl.when(jnp.logical_not(first))
    def _():
        buf[0:halo, :] = buf[ts:ts + halo, :]


def _conv_chunk(hbuf, r0, rc, cw_ref, cb_ref, lg_ref, lb_ref):
    back = 8 * ((CONV_WIDTH - 1) // 8)
    acc = jnp.broadcast_to(cb_ref[...], (rc, GROUP))
    for r in range(8):
        lo = CONV_HALO + r0 - back - r
        hr = hbuf[lo:lo + back + rc, :]
        part = None
        for q in range(back // 8 + 1):
            d = 8 * q + r
            if d < CONV_WIDTH:
                j = CONV_WIDTH - 1 - d
                term = cw_ref[j:j + 1, :] * hr[back - 8 * q:back - 8 * q + rc]
                part = term if part is None else part + term
        acc = acc + part
    mu = jnp.mean(acc, axis=-1, keepdims=True)
    d = acc - mu
    var = jnp.mean(d * d, axis=-1, keepdims=True)
    return _silu(d * lax.rsqrt(var + 1e-5) * lg_ref[...] + lb_ref[...])


def _pool_chunk(ubuf, r0, rc, t0):
    lane_grp = lax.broadcasted_iota(jnp.int32, (rc, GROUP), 1) // (GROUP // len(POOL_WINDOWS))
    t1 = (t0 + r0 + 1 + lax.broadcasted_iota(jnp.int32, (rc, 1), 0)).astype(F32)
    run = ubuf[POOL_HALO + r0 - 8:POOL_HALO + r0 + rc, :]
    u0 = run[8:]
    totals = {}
    for dlt in range(1, 8):
        run = run + ubuf[POOL_HALO + r0 - 8 - dlt:POOL_HALO + r0 + rc - dlt, :]
        totals[dlt + 1] = run[8:]
    totals[16] = run[8:] + run[:rc]
    pooled = jnp.zeros((rc, GROUP), F32)
    for gi, w in enumerate(POOL_WINDOWS):
        pooled = jnp.where(lane_grp == gi, totals[w] / jnp.minimum(t1, float(w)) - u0, pooled)
    return pooled


def _split_dot(x, p):
    hi = x.astype(BF16)
    lo = (x - hi.astype(F32)).astype(BF16)
    return (jnp.dot(hi, p, preferred_element_type=F32) + jnp.dot(lo, p, preferred_element_type=F32))


def _retention_tile(q, k, v, g, state, dm_ref, qd_ref, kd_ref, cd_ref, bd_ref, gg_ref, gb_ref, y_ref):
    C = RET_CHUNK
    chunks = [slice(r, r + C) for r in range(0, q.shape[0], C)]
    head = _lane_head((1, GROUP), 1)
    nt = (((1,), (1,)), ((), ()))
    masks = [head == h for h in range(N_HEADS)]
    bd = bd_ref[...]
    p = (bd * (1.0 / HEAD_DIM)).astype(BF16)
    scores = [[lax.dot_general(jnp.where(mh, q[rows], jnp.zeros_like(q[rows])), k[rows], nt,
                               preferred_element_type=F32) for mh in masks] for rows in chunks]
    o = []
    for rows in chunks:
        o.append(jnp.dot(q[rows], state[...].astype(BF16), preferred_element_type=F32) * qd_ref[...])
        kdec = (k[rows].astype(F32) * kd_ref[...]).astype(BF16)
        upd = lax.dot_general(kdec, v[rows], (((0,), (0,)), ((), ())), preferred_element_type=F32)
        state[...] = state[...] * cd_ref[...] + bd * upd
    for c, rows in enumerate(chunks):
        for h, mh in enumerate(masks):
            oh = jnp.dot((scores[c][h] * dm_ref[h]).astype(BF16), v[rows], preferred_element_type=F32)
            o[c] = o[c] + jnp.where(mh, oh, 0.0)
    d = [oc - _split_dot(oc, p) for oc in o]
    var = [_split_dot(dc * dc, p) for dc in d]
    for c, rows in enumerate(chunks):
        on = d[c] * lax.rsqrt(var[c] + 1e-5) * gg_ref[...] + gb_ref[...]
        y_ref[rows, :] = (_silu(g[rows]) * on).astype(BF16)


def _in_proj_kernel(per_seq, x_ref, g_ref, w_ref, cr_ref, sr_ref, cm_ref, sm_ref,
                    cw_ref, cb_ref, lg_ref, lb_ref, wp_ref, pb_ref, ps_ref,
                    dm_ref, qd_ref, kd_ref, cd_ref, bd_ref, gg_ref, gb_ref,
                    ya_ref, yb_ref, mqt_ref, mk_ref, mvt_ref, km_ref, yd_ref,
                    hbuf, ubuf, pool_scr, state):
    G = GROUP
    ts = x_ref.shape[0]
    s = pl.program_id(0) % per_seq
    _shift_halo(hbuf, CONV_HALO, ts, s == 0)
    _shift_halo(ubuf, POOL_HALO, ts, s == 0)

    @pl.when(s == 0)
    def _():
        state[...] = jnp.zeros_like(state)

    hb = _rms(x_ref[...], g_ref[...]).astype(BF16)

    def proj(i, n=1):
        return jnp.dot(hb, w_ref[:, i * G:(i + n) * G], preferred_element_type=F32)

    uc = proj(0, 2)
    hbuf[CONV_HALO:, :] = uc[:, :G] * jax.nn.sigmoid(uc[:, G:])
    ubuf[POOL_HALO:, :] = proj(9)
    cr, sr = cr_ref[...], sr_ref[...]
    rq = _rotate_half(proj(2), cr, sr, HEAD_DIM // 2).astype(BF16)
    rk = (_rotate_half(proj(3), cr, sr, HEAD_DIM // 2) * (HEAD_DIM ** -0.5)).astype(BF16)
    rv = proj(4).astype(BF16)
    rg = proj(5)
    cm, sm = cm_ref[...], sm_ref[...]
    mq = _rotate_half(proj(6), cm, sm, ROT_DIM // 2)
    mk = _rotate_half(proj(7), cm, sm, ROT_DIM // 2)
    mv = proj(8)
    mk_ref[...] = mk.astype(BF16)
    nblk = mq.shape[0] // MOBA_BLOCK
    for j in range(nblk):
        rows = slice(j * MOBA_BLOCK, (j + 1) * MOBA_BLOCK)
        mqt_ref[j] = mq[rows].T
        vt = mv[rows].T.astype(BF16)
        for h in range(N_HEADS):
            r = h * VT_ROWS
            mvt_ref[j, r:r + HEAD_DIM, :] = vt[h * HEAD_DIM:(h + 1) * HEAD_DIM]
            mvt_ref[j, r + HEAD_DIM:r + VT_ROWS, :] = jnp.ones((VT_ROWS - HEAD_DIM, MOBA_BLOCK), BF16)
        km_ref[0, j:j + 1, :] = jnp.mean(mk[rows], axis=0, keepdims=True)

    _retention_tile(rq, rk, rv, rg, state, dm_ref, qd_ref, kd_ref, cd_ref, bd_ref, gg_ref, gb_ref, yb_ref)

    rc = LOCAL_CHUNK
    for r0 in range(0, ts, rc):
        ya_ref[r0:r0 + rc, :] = _conv_chunk(hbuf, r0, rc, cw_ref, cb_ref, lg_ref, lb_ref).astype(BF16)
        pool_scr[r0:r0 + rc, :] = _pool_chunk(ubuf, r0, rc, s * ts).astype(BF16)
    y = jnp.dot(pool_scr[...], wp_ref[...], preferred_element_type=F32)
    yd_ref[...] = ((y + pb_ref[...]) * ps_ref[...]).astype(BF16)


def _in_proj(x2, layer, g, w_in, tabs, local, rtabs, gn, seq):
    T, D = x2.shape
    tm = ROW_TILE
    nt = T // tm
    per_seq = seq // tm
    nblk = tm // MOBA_BLOCK
    G = GROUP
    row = lambda i: (i, 0)
    tab = pl.BlockSpec((tm, G), lambda i: (i % per_seq, 0))
    lay = lambda a: _layer_spec(a, layer)
    bfo = jax.ShapeDtypeStruct((T, G), BF16)
    blk3 = lambda rows, dt: jax.ShapeDtypeStruct((T // MOBA_BLOCK, rows, MOBA_BLOCK), dt)
    o_row = pl.BlockSpec((tm, G), row)
    o_blk = lambda rows: pl.BlockSpec((nblk, rows, MOBA_BLOCK), lambda i: (i, 0, 0))
    return pl.pallas_call(
        functools.partial(_in_proj_kernel, per_seq),
        grid=(nt,),
        in_specs=[pl.BlockSpec((tm, D), row), lay(g), lay(w_in), tab, tab, tab, tab,
                  *[lay(a) for a in local],
                  *[_const_spec(a.shape) for a in rtabs],
                  *[lay(a) for a in gn]],
        out_specs=[o_row, o_row, o_blk(G), o_row, o_blk(N_HEADS * VT_ROWS),
                   pl.BlockSpec((1, nblk, G), lambda i: (i, 0, 0)), o_row],
        out_shape=[bfo, bfo, blk3(G, F32), bfo, blk3(N_HEADS * VT_ROWS, BF16),
                   jax.ShapeDtypeStruct((nt, nblk, G), F32), bfo],
        scratch_shapes=[pltpu.VMEM((CONV_HALO + tm, G), F32), pltpu.VMEM((POOL_HALO + tm, G), F32),
                        pltpu.VMEM((tm, G), BF16), pltpu.VMEM((G, G), F32)],
        compiler_params=_cparams(("arbitrary",)),
        name="in_proj",
    )(x2, g, w_in, *tabs, *local, *rtabs, *gn)


def _moba_kernel(qt_ref, k_ref, vt_ref, km_ref, y_ref, sel_ref, qh_ref, m_ref, acc_ref, ot_ref, sa_ref, sb_ref):
    a = pl.program_id(1)
    nb = km_ref.shape[1]
    Q = MOBA_BLOCK
    chains = [(t, h) for t in range(MOBA_TILES) for h in range(N_HEADS)]
    cid = lambda t, h: t * N_HEADS + h
    ch_head = _lane_head((GROUP, 1), 0)
    km = km_ref[0]
    km_head = _lane_head((1, GROUP), 1)
    blk = lax.broadcasted_iota(jnp.int32, (nb, Q), 0)

    km_heads = jnp.concatenate([jnp.where(km_head == h, km, 0.0) for h in range(N_HEADS)], axis=0)
    for t in range(MOBA_TILES):
        qt = qt_ref[t]
        qs = (qt * (HEAD_DIM ** -0.5 * LOG2E)).astype(BF16)
        gate_all = jnp.dot(km_heads, qt, preferred_element_type=F32, precision=lax.Precision.HIGHEST)
        past = blk < MOBA_TILES * a + t
        for h in range(N_HEADS):
            gate = jnp.where(past, gate_all[h * nb:(h + 1) * nb], -jnp.inf)
            sel = jnp.zeros((nb, Q), F32)
            for _ in range(MOBA_TOPK):
                top = jnp.max(gate, axis=0, keepdims=True)
                first = jnp.min(jnp.where(gate == top, blk, nb), axis=0, keepdims=True)
                pick = (blk == first) & past
                sel = jnp.where(pick, 1.0, sel)
                gate = jnp.where(pick, -jnp.inf, gate)
            sel_ref[cid(t, h) * nb:(cid(t, h) + 1) * nb, :] = sel
            qh_ref[cid(t, h)] = jnp.where(ch_head == h, qs, jnp.zeros_like(qs))

    m_ref[...] = jnp.full(m_ref.shape, -BIG, F32)
    acc_ref[...] = jnp.zeros(acc_ref.shape, F32)

    def score(j, s_ref, which):
        kb = k_ref[j]
        for t, h in which:
            s_ref[cid(t, h)] = jnp.dot(kb, qh_ref[cid(t, h)], preferred_element_type=F32)

    def absorb(j, s_ref, which, own_tile=None):
        causal = None
        for t, h in which:
            c = cid(t, h)
            m = m_ref[c:c + 1, :]
            s = s_ref[c]
            if t == own_tile:
                if causal is None:
                    causal = (lax.broadcasted_iota(jnp.int32, (Q, Q), 0)
                              <= lax.broadcasted_iota(jnp.int32, (Q, Q), 1))
                s = jnp.where(causal, s, NEG)
                m_new = jnp.maximum(m, jnp.max(s, axis=0, keepdims=True))
                shift = m_new
            else:
                on = sel_ref[pl.ds(c * nb + j, 1), :] > 0.0
                m_new = jnp.where(on, jnp.maximum(m, jnp.max(s, axis=0, keepdims=True)), m)
                shift = jnp.where(on, m_new, BIG)
            p = jnp.exp2(s - shift).astype(BF16)
            vt = vt_ref[j, h * VT_ROWS:(h + 1) * VT_ROWS, :]
            acc_ref[c] = jnp.exp2(m - m_new) * acc_ref[c] + jnp.dot(vt, p, preferred_element_type=F32)
            m_ref[c:c + 1, :] = m_new

    def body(i, carry):
        score(2 * i + 1, sb_ref, chains)
        absorb(2 * i, sa_ref, chains)
        score(2 * i + 2, sa_ref, chains)
        absorb(2 * i + 1, sb_ref, chains)
        return carry

    first = MOBA_TILES * a
    score(0, sa_ref, chains)
    lax.fori_loop(0, first // 2, body, 0)
    bufs = (sa_ref, sb_ref)
    for k in range(MOBA_TILES):
        if k + 1 < MOBA_TILES:
            score(first + k + 1, bufs[(k + 1) % 2], [c for c in chains if c[0] > k])
        absorb(first + k, bufs[k % 2], [c for c in chains if c[0] >= k], own_tile=k)

    for t in range(MOBA_TILES):
        for h in range(N_HEADS):
            acc = acc_ref[cid(t, h)]
            ot_ref[h * HEAD_DIM:(h + 1) * HEAD_DIM, :] = acc[:HEAD_DIM] / acc[HEAD_DIM:HEAD_DIM + 1]
        y_ref[t * Q:(t + 1) * Q, :] = ot_ref[...].T.astype(BF16)


def _moba(mqt, mk3, mvt, km, batch, seq):
    nb = seq // MOBA_BLOCK
    B = MOBA_BLOCK
    G = GROUP
    nt = MOBA_TILES
    assert nt % 2 == 0 and nb % nt == 0
    steps = nb // nt
    nc = nt * N_HEADS
    return pl.pallas_call(
        _moba_kernel,
        grid=(batch, steps),
        in_specs=[pl.BlockSpec((nt, G, B), lambda b, i: (b * steps + i, 0, 0)),
                  pl.BlockSpec((nb, B, G), lambda b, i: (b, 0, 0)),
                  pl.BlockSpec((nb, N_HEADS * VT_ROWS, B), lambda b, i: (b, 0, 0)),
                  pl.BlockSpec((1, nb, G), lambda b, i: (b, 0, 0))],
        out_specs=pl.BlockSpec((nt * B, G), lambda b, i: (b * steps + i, 0)),
        out_shape=jax.ShapeDtypeStruct((batch * seq, G), BF16),
        scratch_shapes=[pltpu.VMEM((nc * nb, B), F32),
                        pltpu.VMEM((nc, G, B), BF16),
                        pltpu.VMEM((nc, B), F32),
                        pltpu.VMEM((nc, VT_ROWS, B), F32),
                        pltpu.VMEM((G, B), F32),
                        pltpu.VMEM((nc, B, B), F32),
                        pltpu.VMEM((nc, B, B), F32)],
        compiler_params=_cparams(("arbitrary", "arbitrary")),
        name="moba",
    )(mqt, mk3, mvt, km)


def _post_kernel(x_ref, ya_ref, yb_ref, yc_ref, yd_ref, wo_ref, gpo_ref, gfi_ref,
                 wg_ref, wu_ref, wd_ref, gfo_ref, o_ref, f_scr):
    G = GROUP
    y = jnp.dot(ya_ref[...], wo_ref[0:G, :], preferred_element_type=F32)
    y = y + jnp.dot(yb_ref[...], wo_ref[G:2 * G, :], preferred_element_type=F32)
    y = y + jnp.dot(yc_ref[...], wo_ref[2 * G:3 * G, :], preferred_element_type=F32)
    y = y + jnp.dot(yd_ref[...], wo_ref[3 * G:4 * G, :], preferred_element_type=F32)
    x1 = x_ref[...] + _rms(y, gpo_ref[...])
    hb = _rms(x1, gfi_ref[...]).astype(BF16)
    d_ff = wg_ref.shape[1]
    for c in range(0, d_ff, FF_CHUNK):
        cols = slice(c, c + FF_CHUNK)
        gt = jnp.dot(hb, wg_ref[:, cols], preferred_element_type=F32)
        up = jnp.dot(hb, wu_ref[:, cols], preferred_element_type=F32)
        f_scr[:, cols] = (_silu(gt) * up).astype(BF16)
    f = jnp.dot(f_scr[...], wd_ref[...], preferred_element_type=F32)
    o_ref[...] = x1 + _rms(f, gfo_ref[...])


def _post(x2, layer, ya, yb, yc, yd, params):
    T, D = x2.shape
    tm = POST_TILE
    G = GROUP
    d_ff = params[3].shape[-1]
    row = lambda i: (i, 0)
    xs = pl.BlockSpec((tm, D), row)
    ys = pl.BlockSpec((tm, G), row)
    return pl.pallas_call(
        _post_kernel,
        grid=(T // tm,),
        in_specs=[xs, ys, ys, ys, ys, *[_layer_spec(a, layer) for a in params]],
        out_specs=xs,
        out_shape=jax.ShapeDtypeStruct((T, D), F32),
        scratch_shapes=[pltpu.VMEM((tm, d_ff), BF16)],
        compiler_params=_cparams(("arbitrary",)),
        name="out_proj_ffn",
    )(x2, ya, yb, yc, yd, *params)


def _rotary_tables(seq, rot_dim, theta):
    pos = jnp.arange(seq, dtype=F32)
    inv = 1.0 / (theta ** (jnp.arange(0, rot_dim, 2, dtype=F32) / rot_dim))
    ang = pos[:, None] * inv[None, :]
    cos, sin = jnp.cos(ang), jnp.sin(ang)
    rest = HEAD_DIM - rot_dim
    cos_h = jnp.concatenate([cos, cos, jnp.ones((seq, rest), F32)], axis=-1)
    sin_h = jnp.concatenate([-sin, sin, jnp.zeros((seq, rest), F32)], axis=-1)
    return jnp.tile(cos_h, (1, N_HEADS)), jnp.tile(sin_h, (1, N_HEADS))


def _retention_tables():
    c = RET_CHUNK
    log_gamma = jnp.log(1.0 - 2.0 ** (-5.0 - jnp.arange(N_HEADS, dtype=F32)))
    idx = jnp.arange(c, dtype=F32)
    diff = idx[:, None] - idx[None, :]
    dm = jnp.where(diff[None] >= 0, jnp.exp(jnp.maximum(diff, 0.0)[None] * log_gamma[:, None, None]), 0.0)
    lanes = lambda t: jnp.repeat(t, HEAD_DIM, axis=-1)
    qd = lanes(jnp.exp((idx + 1.0)[:, None] * log_gamma[None, :]))
    kd = lanes(jnp.exp((c - 1.0 - idx)[:, None] * log_gamma[None, :]))
    cd = lanes(jnp.exp(c * log_gamma)[None, :])
    head = jnp.arange(GROUP) // HEAD_DIM
    bd = (head[:, None] == head[None, :]).astype(F32)
    return dm, qd, kd, cd, bd


def _block_diag(w):
    n, c, _ = w.shape
    eye = jnp.eye(n, dtype=w.dtype)
    return (eye[:, None, :, None] * w[:, :, None, :]).reshape(n * c, n * c)


def kernel(x, attn_pre_g, w_in, conv_w, conv_b, conv_ln_g, conv_ln_b, ret_gn_g, ret_gn_b, pool_w, pool_b,
           pool_scale, w_out, attn_post_g, ffn_pre_g, w_gate, w_up, w_down, ffn_post_g):
    batch, seq, d_model = x.shape
    depth = w_in.shape[0]
    assert seq % ROW_TILE == 0 and ROW_TILE % LOCAL_CHUNK == 0 and seq % RET_CHUNK == 0 and seq % MOBA_BLOCK == 0
    assert (batch * seq) % POST_TILE == 0
    nb = seq // MOBA_BLOCK
    tabs = _rotary_tables(seq, HEAD_DIM, RET_ROT_THETA) + _rotary_tables(seq, ROT_DIM, ROPE_THETA)
    rtabs = _retention_tables()
    vec = lambda a: a.reshape(depth, 1, -1)
    bf = lambda a: a.astype(BF16)
    in_g, in_w = vec(attn_pre_g), bf(w_in)
    local = (conv_w, vec(conv_b), vec(conv_ln_g), vec(conv_ln_b),
             bf(jax.vmap(_block_diag)(pool_w)), vec(pool_b), vec(pool_scale))
    gn = (vec(ret_gn_g), vec(ret_gn_b))
    post = (bf(w_out), vec(attn_post_g), vec(ffn_pre_g), bf(w_gate), bf(w_up), bf(w_down), vec(ffn_post_g))

    x2 = x.reshape(batch * seq, d_model)
    for l in range(depth):
        ya, yb, mqt, mk, mvt, km, yd = _in_proj(x2, l, in_g, in_w, tabs, local, rtabs, gn, seq)
        yc = _moba(mqt, mk.reshape(batch * nb, MOBA_BLOCK, GROUP), mvt, km.reshape(batch, nb, GROUP), batch, seq)
        x2 = _post(x2, l, ya, yb, yc, yd, post)
    return x2.reshape(batch, seq, d_model)
```

```python
import functools
import math

import jax
import jax.numpy as jnp
from jax import lax
from jax.experimental import pallas as pl
from jax.experimental.pallas import tpu as pltpu

F32 = jnp.float32
BF16 = jnp.bfloat16

GROUP = 256
HEAD_DIM = 64
N_HEADS = GROUP // HEAD_DIM
CONV_WIDTH = 31
RET_ROT_THETA = 10000.0
MOBA_BLOCK = 256
MOBA_TOPK = 3
ROPE_THETA = 500000.0
ROT_DIM = HEAD_DIM // 4
POOL_WINDOWS = (2, 4, 8, 16)
NEG = -1e30
BIG = 1e30
VT_ROWS = HEAD_DIM + 16
LOG2E = math.log2(math.e)

ROW_TILE = 1024
POST_TILE = 1024
POST_SPLIT = 4
LOCAL_CHUNK = 64
CONV_HALO = 32
POOL_HALO = 16
RET_CHUNK = 256
FF_CHUNK = 256
MOBA_TILES = 2
VMEM_LIMIT = 56 * 1024 * 1024


def _cparams(sem):
    return pltpu.CompilerParams(dimension_semantics=sem, vmem_limit_bytes=VMEM_LIMIT)


def _const_spec(shape):
    nd = len(shape)
    return pl.BlockSpec(shape, lambda *_: (0,) * nd, pipeline_mode=pl.Buffered(1))


def _layer_spec(stacked, layer):
    shape = stacked.shape[1:]
    nd = len(shape)
    return pl.BlockSpec((None,) + shape, lambda *_: (layer,) + (0,) * nd, pipeline_mode=pl.Buffered(1))


def _rms(x, g):
    return x * lax.rsqrt(jnp.mean(x * x, axis=-1, keepdims=True) + 1e-6) * g


def _silu(x):
    return x * jax.nn.sigmoid(x)


def _lane_head(shape, axis):
    return lax.broadcasted_iota(jnp.int32, shape, axis) // HEAD_DIM


def _rotate_half(x, cos, sin_signed, half):
    n = x.shape[-1]
    lane = lax.broadcasted_iota(jnp.int32, x.shape, 1) % HEAD_DIM
    partner = jnp.where(lane < half, pltpu.roll(x, n - half, 1), pltpu.roll(x, half, 1))
    return x * cos + partner * sin_signed


def _shift_halo(buf, halo, ts, first):
    @pl.when(first)
    def _():
        buf[0:halo, :] = jnp.zeros((halo, GROUP), F32)

    @pl.when(jnp.logical_not(first))
    def _():
        buf[0:halo, :] = buf[ts:ts + halo, :]


def _conv_chunk(hbuf, r0, rc, cw_ref, cb_ref, lg_ref, lb_ref):
    back = 8 * ((CONV_WIDTH - 1) // 8)
    acc = jnp.broadcast_to(cb_ref[...], (rc, GROUP))
    for r in range(8):
        lo = CONV_HALO + r0 - back - r
        hr = hbuf[lo:lo + back + rc, :]
        part = None
        for q in range(back // 8 + 1):
            d = 8 * q + r
            if d < CONV_WIDTH:
                j = CONV_WIDTH - 1 - d
                term = cw_ref[j:j + 1, :] * hr[back - 8 * q:back - 8 * q + rc]
                part = term if part is None else part + term
        acc = acc + part
    mu = jnp.mean(acc, axis=-1, keepdims=True)
    d = acc - mu
    var = jnp.mean(d * d, axis=-1, keepdims=True)
    return _silu(d * lax.rsqrt(var + 1e-5) * lg_ref[...] + lb_ref[...])


def _pool_chunk(ubuf, r0, rc, t0):
    lane_grp = lax.broadcasted_iota(jnp.int32, (rc, GROUP), 1) // (GROUP // len(POOL_WINDOWS))
    t1 = (t0 + r0 + 1 + lax.broadcasted_iota(jnp.int32, (rc, 1), 0)).astype(F32)
    run = ubuf[POOL_HALO + r0 - 8:POOL_HALO + r0 + rc, :]
    u0 = run[8:]
    totals = {}
    for dlt in range(1, 8):
        run = run + ubuf[POOL_HALO + r0 - 8 - dlt:POOL_HALO + r0 + rc - dlt, :]
        totals[dlt + 1] = run[8:]
    totals[16] = run[8:] + run[:rc]
    pooled = jnp.zeros((rc, GROUP), F32)
    for gi, w in enumerate(POOL_WINDOWS):
        pooled = jnp.where(lane_grp == gi, totals[w] / jnp.minimum(t1, float(w)) - u0, pooled)
    return pooled


def _split_dot(x, p):
    hi = x.astype(BF16)
    lo = (x - hi.astype(F32)).astype(BF16)
    return (jnp.dot(hi, p, preferred_element_type=F32) + jnp.dot(lo, p, preferred_element_type=F32))


def _retention_tile(q, k, v, g, state, dm_ref, qd_ref, kd_ref, cd_ref, bd_ref, gg_ref, gb_ref, y_ref):
    C = RET_CHUNK
    chunks = [slice(r, r + C) for r in range(0, q.shape[0], C)]
    head = _lane_head((1, GROUP), 1)
    nt = (((1,), (1,)), ((), ()))
    masks = [head == h for h in range(N_HEADS)]
    bd = bd_ref[...]
    p = (bd * (1.0 / HEAD_DIM)).astype(BF16)
    scores = [[lax.dot_general(jnp.where(mh, q[rows], jnp.zeros_like(q[rows])), k[rows], nt,
                               preferred_element_type=F32) for mh in masks] for rows in chunks]
    o = []
    for rows in chunks:
        o.append(jnp.dot(q[rows], state[...].astype(BF16), preferred_element_type=F32) * qd_ref[...])
        kdec = (k[rows].astype(F32) * kd_ref[...]).astype(BF16)
        upd = lax.dot_general(kdec, v[rows], (((0,), (0,)), ((), ())), preferred_element_type=F32)
        state[...] = state[...] * cd_ref[...] + bd * upd
    for c, rows in enumerate(chunks):
        for h, mh in enumerate(masks):
            oh = jnp.dot((scores[c][h] * dm_ref[h]).astype(BF16), v[rows], preferred_element_type=F32)
            o[c] = o[c] + jnp.where(mh, oh, 0.0)
    d = [oc - _split_dot(oc, p) for oc in o]
    var = [_split_dot(dc * dc, p) for dc in d]
    for c, rows in enumerate(chunks):
        on = d[c] * lax.rsqrt(var[c] + 1e-5) * gg_ref[...] + gb_ref[...]
        y_ref[rows, :] = (_silu(g[rows]) * on).astype(BF16)


def _in_proj_kernel(per_seq, x_ref, g_ref, w_ref, cr_ref, sr_ref, cm_ref, sm_ref,
                    cw_ref, cb_ref, lg_ref, lb_ref, wp_ref, pb_ref, ps_ref,
                    dm_ref, qd_ref, kd_ref, cd_ref, bd_ref, gg_ref, gb_ref,
                    ya_ref, yb_ref, mqt_ref, mk_ref, mvt_ref, km_ref, yd_ref,
                    hbuf, ubuf, pool_scr, state):
    G = GROUP
    ts = x_ref.shape[0]
    s = pl.program_id(0) % per_seq
    _shift_halo(hbuf, CONV_HALO, ts, s == 0)
    _shift_halo(ubuf, POOL_HALO, ts, s == 0)

    @pl.when(s == 0)
    def _():
        state[...] = jnp.zeros_like(state)

    hb = _rms(x_ref[...], g_ref[...]).astype(BF16)

    def proj(i, n=1):
        return jnp.dot(hb, w_ref[:, i * G:(i + n) * G], preferred_element_type=F32)

    uc = proj(0, 2)
    hbuf[CONV_HALO:, :] = uc[:, :G] * jax.nn.sigmoid(uc[:, G:])
    ubuf[POOL_HALO:, :] = proj(9)
    cr, sr = cr_ref[...], sr_ref[...]
    rq = _rotate_half(proj(2), cr, sr, HEAD_DIM // 2).astype(BF16)
    rk = (_rotate_half(proj(3), cr, sr, HEAD_DIM // 2) * (HEAD_DIM ** -0.5)).astype(BF16)
    rv = proj(4).astype(BF16)
    rg = proj(5)
    cm, sm = cm_ref[...], sm_ref[...]
    mq = _rotate_half(proj(6), cm, sm, ROT_DIM // 2)
    mk = _rotate_half(proj(7), cm, sm, ROT_DIM // 2)
    mv = proj(8)
    mk_ref[...] = mk.astype(BF16)
    nblk = mq.shape[0] // MOBA_BLOCK
    for j in range(nblk):
        rows = slice(j * MOBA_BLOCK, (j + 1) * MOBA_BLOCK)
        mqt_ref[j] = mq[rows].T
        vt = mv[rows].T.astype(BF16)
        for h in range(N_HEADS):
            r = h * VT_ROWS
            mvt_ref[j, r:r + HEAD_DIM, :] = vt[h * HEAD_DIM:(h + 1) * HEAD_DIM]
            mvt_ref[j, r + HEAD_DIM:r + VT_ROWS, :] = jnp.ones((VT_ROWS - HEAD_DIM, MOBA_BLOCK), BF16)
        km_ref[0, j:j + 1, :] = jnp.mean(mk[rows], axis=0, keepdims=True)

    _retention_tile(rq, rk, rv, rg, state, dm_ref, qd_ref, kd_ref, cd_ref, bd_ref, gg_ref, gb_ref, yb_ref)

    rc = LOCAL_CHUNK
    for r0 in range(0, ts, rc):
        ya_ref[r0:r0 + rc, :] = _conv_chunk(hbuf, r0, rc, cw_ref, cb_ref, lg_ref, lb_ref).astype(BF16)
        pool_scr[r0:r0 + rc, :] = _pool_chunk(ubuf, r0, rc, s * ts).astype(BF16)
    y = jnp.dot(pool_scr[...], wp_ref[...], preferred_element_type=F32)
    yd_ref[...] = ((y + pb_ref[...]) * ps_ref[...]).astype(BF16)


def _in_proj(x2, layer, g, w_in, tabs, local, rtabs, gn, seq):
    T, D = x2.shape
    tm = ROW_TILE
    nt = T // tm
    per_seq = seq // tm
    nblk = tm // MOBA_BLOCK
    G = GROUP
    row = lambda i: (i, 0)
    tab = pl.BlockSpec((tm, G), lambda i: (i % per_seq, 0))
    lay = lambda a: _layer_spec(a, layer)
    bfo = jax.ShapeDtypeStruct((T, G), BF16)
    blk3 = lambda rows, dt: jax.ShapeDtypeStruct((T // MOBA_BLOCK, rows, MOBA_BLOCK), dt)
    o_row = pl.BlockSpec((tm, G), row)
    o_blk = lambda rows: pl.BlockSpec((nblk, rows, MOBA_BLOCK), lambda i: (i, 0, 0))
    return pl.pallas_call(
        functools.partial(_in_proj_kernel, per_seq),
        grid=(nt,),
        in_specs=[pl.BlockSpec((tm, D), row), lay(g), lay(w_in), tab, tab, tab, tab,
                  *[lay(a) for a in local],
                  *[_const_spec(a.shape) for a in rtabs],
                  *[lay(a) for a in gn]],
        out_specs=[o_row, o_row, o_blk(G), o_row, o_blk(N_HEADS * VT_ROWS),
                   pl.BlockSpec((1, nblk, G), lambda i: (i, 0, 0)), o_row],
        out_shape=[bfo, bfo, blk3(G, F32), bfo, blk3(N_HEADS * VT_ROWS, BF16),
                   jax.ShapeDtypeStruct((nt, nblk, G), F32), bfo],
        scratch_shapes=[pltpu.VMEM((CONV_HALO + tm, G), F32), pltpu.VMEM((POOL_HALO + tm, G), F32),
                        pltpu.VMEM((tm, G), BF16), pltpu.VMEM((G, G), F32)],
        compiler_params=_cparams(("arbitrary",)),
        name="in_proj",
    )(x2, g, w_in, *tabs, *local, *rtabs, *gn)


def _moba_kernel(qt_ref, k_ref, vt_ref, km_ref, y_ref, sel_ref, qh_ref, m_ref, acc_ref, ot_ref, sa_ref, sb_ref):
    a = pl.program_id(1)
    nb = km_ref.shape[1]
    Q = MOBA_BLOCK
    chains = [(t, h) for t in range(MOBA_TILES) for h in range(N_HEADS)]
    cid = lambda t, h: t * N_HEADS + h
    ch_head = _lane_head((GROUP, 1), 0)
    km = km_ref[0]
    km_head = _lane_head((1, GROUP), 1)
    blk = lax.broadcasted_iota(jnp.int32, (nb, Q), 0)

    km_heads = jnp.concatenate([jnp.where(km_head == h, km, 0.0) for h in range(N_HEADS)], axis=0)
    for t in range(MOBA_TILES):
        qt = qt_ref[t]
        qs = (qt * (HEAD_DIM ** -0.5 * LOG2E)).astype(BF16)
        gate_all = jnp.dot(km_heads, qt, preferred_element_type=F32, precision=lax.Precision.HIGHEST)
        past = blk < MOBA_TILES * a + t
        for h in range(N_HEADS):
            gate = jnp.where(past, gate_all[h * nb:(h + 1) * nb], -jnp.inf)
            sel = jnp.zeros((nb, Q), F32)
            for _ in range(MOBA_TOPK):
                top = jnp.max(gate, axis=0, keepdims=True)
                first = jnp.min(jnp.where(gate == top, blk, nb), axis=0, keepdims=True)
                pick = (blk == first) & past
                sel = jnp.where(pick, 1.0, sel)
                gate = jnp.where(pick, -jnp.inf, gate)
            sel_ref[cid(t, h) * nb:(cid(t, h) + 1) * nb, :] = sel
            qh_ref[cid(t, h)] = jnp.where(ch_head == h, qs, jnp.zeros_like(qs))

    m_ref[...] = jnp.full(m_ref.shape, -BIG, F32)
    acc_ref[...] = jnp.zeros(acc_ref.shape, F32)

    def score(j, s_ref, which):
        kb = k_ref[j]
        for t, h in which:
            s_ref[cid(t, h)] = jnp.dot(kb, qh_ref[cid(t, h)], preferred_element_type=F32)

    def absorb(j, s_ref, which, own_tile=None):
        causal = None
        for t, h in which:
            c = cid(t, h)
            m = m_ref[c:c + 1, :]
            s = s_ref[c]
            if t == own_tile:
                if causal is None:
                    causal = (lax.broadcasted_iota(jnp.int32, (Q, Q), 0)
                              <= lax.broadcasted_iota(jnp.int32, (Q, Q), 1))
                s = jnp.where(causal, s, NEG)
                m_new = jnp.maximum(m, jnp.max(s, axis=0, keepdims=True))
                shift = m_new
            else:
                on = sel_ref[pl.ds(c * nb + j, 1), :] > 0.0
                m_new = jnp.where(on, jnp.maximum(m, jnp.max(s, axis=0, keepdims=True)), m)
                shift = jnp.where(on, m_new, BIG)
            p = jnp.exp2(s - shift).astype(BF16)
            vt = vt_ref[j, h * VT_ROWS:(h + 1) * VT_ROWS, :]
            acc_ref[c] = jnp.exp2(m - m_new) * acc_ref[c] + jnp.dot(vt, p, preferred_element_type=F32)
            m_ref[c:c + 1, :] = m_new

    def body(i, carry):
        score(2 * i + 1, sb_ref, chains)
        absorb(2 * i, sa_ref, chains)
        score(2 * i + 2, sa_ref, chains)
        absorb(2 * i + 1, sb_ref, chains)
        return carry

    first = MOBA_TILES * a
    score(0, sa_ref, chains)
    lax.fori_loop(0, first // 2, body, 0)
    bufs = (sa_ref, sb_ref)
    for k in range(MOBA_TILES):
        if k + 1 < MOBA_TILES:
            score(first + k + 1, bufs[(k + 1) % 2], [c for c in chains if c[0] > k])
        absorb(first + k, bufs[k % 2], [c for c in chains if c[0] >= k], own_tile=k)

    for t in range(MOBA_TILES):
        for h in range(N_HEADS):
            acc = acc_ref[cid(t, h)]
            ot_ref[h * HEAD_DIM:(h + 1) * HEAD_DIM, :] = acc[:HEAD_DIM] / acc[HEAD_DIM:HEAD_DIM + 1]
        y_ref[t * Q:(t + 1) * Q, :] = ot_ref[...].T.astype(BF16)


def _moba(mqt, mk3, mvt, km, batch, seq):
    nb = seq // MOBA_BLOCK
    B = MOBA_BLOCK
    G = GROUP
    nt = MOBA_TILES
    assert nt % 2 == 0 and nb % nt == 0
    steps = nb // nt
    nc = nt * N_HEADS
    return pl.pallas_call(
        _moba_kernel,
        grid=(batch, steps),
        in_specs=[pl.BlockSpec((nt, G, B), lambda b, i: (b * steps + i, 0, 0)),
                  pl.BlockSpec((nb, B, G), lambda b, i: (b, 0, 0)),
                  pl.BlockSpec((nb, N_HEADS * VT_ROWS, B), lambda b, i: (b, 0, 0)),
                  pl.BlockSpec((1, nb, G), lambda b, i: (b, 0, 0))],
        out_specs=pl.BlockSpec((nt * B, G), lambda b, i: (b * steps + i, 0)),
        out_shape=jax.ShapeDtypeStruct((batch * seq, G), BF16),
        scratch_shapes=[pltpu.VMEM((nc * nb, B), F32),
                        pltpu.VMEM((nc, G, B), BF16),
                        pltpu.VMEM((nc, B), F32),
                        pltpu.VMEM((nc, VT_ROWS, B), F32),
                        pltpu.VMEM((G, B), F32),
                        pltpu.VMEM((nc, B, B), F32),
                        pltpu.VMEM((nc, B, B), F32)],
        compiler_params=_cparams(("arbitrary", "arbitrary")),
        name="moba",
    )(mqt, mk3, mvt, km)


def _post_kernel(x_ref, ya_ref, yb_ref, yc_ref, yd_ref, wo_ref, gpo_ref, gfi_ref,
                 wg_ref, wu_ref, wd_ref, gfo_ref, o_ref, f_scr):
    G = GROUP
    tm = x_ref.shape[0]
    d_ff = wg_ref.shape[1]
    halves = [slice(r, r + tm // POST_SPLIT) for r in range(0, tm, tm // POST_SPLIT)]
    ys = []
    for rows in halves:
        y = jnp.dot(ya_ref[rows, :], wo_ref[0:G, :], preferred_element_type=F32)
        y = y + jnp.dot(yb_ref[rows, :], wo_ref[G:2 * G, :], preferred_element_type=F32)
        y = y + jnp.dot(yc_ref[rows, :], wo_ref[2 * G:3 * G, :], preferred_element_type=F32)
        ys.append(y + jnp.dot(yd_ref[rows, :], wo_ref[3 * G:4 * G, :], preferred_element_type=F32))
    for rows, y in zip(halves, ys):
        x1 = x_ref[rows, :] + _rms(y, gpo_ref[...])
        hb = _rms(x1, gfi_ref[...]).astype(BF16)
        for c in range(0, d_ff, FF_CHUNK):
            cols = slice(c, c + FF_CHUNK)
            gt = jnp.dot(hb, wg_ref[:, cols], preferred_element_type=F32)
            up = jnp.dot(hb, wu_ref[:, cols], preferred_element_type=F32)
            f_scr[rows, cols] = (_silu(gt) * up).astype(BF16)
        f = jnp.dot(f_scr[rows, :], wd_ref[...], preferred_element_type=F32)
        o_ref[rows, :] = x1 + _rms(f, gfo_ref[...])


def _post(x2, layer, ya, yb, yc, yd, params):
    T, D = x2.shape
    tm = POST_TILE
    G = GROUP
    d_ff = params[3].shape[-1]
    row = lambda i: (i, 0)
    xs = pl.BlockSpec((tm, D), row)
    ys = pl.BlockSpec((tm, G), row)
    return pl.pallas_call(
        _post_kernel,
        grid=(T // tm,),
        in_specs=[xs, ys, ys, ys, ys, *[_layer_spec(a, layer) for a in params]],
        out_specs=xs,
        out_shape=jax.ShapeDtypeStruct((T, D), F32),
        scratch_shapes=[pltpu.VMEM((tm, d_ff), BF16)],
        compiler_params=_cparams(("arbitrary",)),
        name="out_proj_ffn",
    )(x2, ya, yb, yc, yd, *params)


def _rotary_tables(seq, rot_dim, theta):
    pos = jnp.arange(seq, dtype=F32)
    inv = 1.0 / (theta ** (jnp.arange(0, rot_dim, 2, dtype=F32) / rot_dim))
    ang = pos[:, None] * inv[None, :]
    cos, sin = jnp.cos(ang), jnp.sin(ang)
    rest = HEAD_DIM - rot_dim
    cos_h = jnp.concatenate([cos, cos, jnp.ones((seq, rest), F32)], axis=-1)
    sin_h = jnp.concatenate([-sin, sin, jnp.zeros((seq, rest), F32)], axis=-1)
    return jnp.tile(cos_h, (1, N_HEADS)), jnp.tile(sin_h, (1, N_HEADS))


def _retention_tables():
    c = RET_CHUNK
    log_gamma = jnp.log(1.0 - 2.0 ** (-5.0 - jnp.arange(N_HEADS, dtype=F32)))
    idx = jnp.arange(c, dtype=F32)
    diff = idx[:, None] - idx[None, :]
    dm = jnp.where(diff[None] >= 0, jnp.exp(jnp.maximum(diff, 0.0)[None] * log_gamma[:, None, None]), 0.0)
    lanes = lambda t: jnp.repeat(t, HEAD_DIM, axis=-1)
    qd = lanes(jnp.exp((idx + 1.0)[:, None] * log_gamma[None, :]))
    kd = lanes(jnp.exp((c - 1.0 - idx)[:, None] * log_gamma[None, :]))
    cd = lanes(jnp.exp(c * log_gamma)[None, :])
    head = jnp.arange(GROUP) // HEAD_DIM
    bd = (head[:, None] == head[None, :]).astype(F32)
    return dm, qd, kd, cd, bd


def _block_diag(w):
    n, c, _ = w.shape
    eye = jnp.eye(n, dtype=w.dtype)
    return (eye[:, None, :, None] * w[:, :, None, :]).reshape(n * c, n * c)


def kernel(x, attn_pre_g, w_in, conv_w, conv_b, conv_ln_g, conv_ln_b, ret_gn_g, ret_gn_b, pool_w, pool_b,
           pool_scale, w_out, attn_post_g, ffn_pre_g, w_gate, w_up, w_down, ffn_post_g):
    batch, seq, d_model = x.shape
    depth = w_in.shape[0]
    assert seq % ROW_TILE == 0 and ROW_TILE % LOCAL_CHUNK == 0 and seq % RET_CHUNK == 0 and seq % MOBA_BLOCK == 0
    assert (batch * seq) % POST_TILE == 0
    nb = seq // MOBA_BLOCK
    tabs = _rotary_tables(seq, HEAD_DIM, RET_ROT_THETA) + _rotary_tables(seq, ROT_DIM, ROPE_THETA)
    rtabs = _retention_tables()
    vec = lambda a: a.reshape(depth, 1, -1)
    bf = lambda a: a.astype(BF16)
    in_g, in_w = vec(attn_pre_g), bf(w_in)
    local = (conv_w, vec(conv_b), vec(conv_ln_g), vec(conv_ln_b),
             bf(jax.vmap(_block_diag)(pool_w)), vec(pool_b), vec(pool_scale))
    gn = (vec(ret_gn_g), vec(ret_gn_b))
    post = (bf(w_out), vec(attn_post_g), vec(ffn_pre_g), bf(w_gate), bf(w_up), bf(w_down), vec(ffn_post_g))

    x2 = x.reshape(batch * seq, d_model)
    for l in range(depth):
        ya, yb, mqt, mk, mvt, km, yd = _in_proj(x2, l, in_g, in_w, tabs, local, rtabs, gn, seq)
        yc = _moba(mqt, mk.reshape(batch * nb, MOBA_BLOCK, GROUP), mvt, km.reshape(batch, nb, GROUP), batch, seq)
        x2 = _post(x2, l, ya, yb, yc, yd, post)
    return x2.reshape(batch, seq, d_model)
```

```python
import functools
import math

import jax
import jax.numpy as jnp
from jax import lax
from jax.experimental import pallas as pl
from jax.experimental.pallas import tpu as pltpu

F32 = jnp.float32
BF16 = jnp.bfloat16

GROUP = 256
HEAD_DIM = 64
N_HEADS = GROUP // HEAD_DIM
CONV_WIDTH = 31
RET_ROT_THETA = 10000.0
MOBA_BLOCK = 256
MOBA_TOPK = 3
ROPE_THETA = 500000.0
ROT_DIM = HEAD_DIM // 4
POOL_WINDOWS = (2, 4, 8, 16)
NEG = -1e30
BIG = 1e30
VT_ROWS = HEAD_DIM + 16
LOG2E = math.log2(math.e)

ROW_TILE = 1024
POST_TILE = 1024
POST_SPLIT = 4
LOCAL_CHUNK = 128
CONV_HALO = 32
POOL_HALO = 16
RET_CHUNK = 256
FF_CHUNK = 256
MOBA_TILES = 2
VMEM_LIMIT = 56 * 1024 * 1024


def _cparams(sem):
    return pltpu.CompilerParams(dimension_semantics=sem, vmem_limit_bytes=VMEM_LIMIT)


def _const_spec(shape):
    nd = len(shape)
    return pl.BlockSpec(shape, lambda *_: (0,) * nd, pipeline_mode=pl.Buffered(1))


def _layer_spec(stacked, layer):
    shape = stacked.shape[1:]
    nd = len(shape)
    return pl.BlockSpec((None,) + shape, lambda *_: (layer,) + (0,) * nd, pipeline_mode=pl.Buffered(1))


def _rms(x, g):
    return x * lax.rsqrt(jnp.mean(x * x, axis=-1, keepdims=True) + 1e-6) * g


def _silu(x):
    return x * jax.nn.sigmoid(x)


def _lane_head(shape, axis):
    return lax.broadcasted_iota(jnp.int32, shape, axis) // HEAD_DIM


def _rotate_half(x, cos, sin_signed, half):
    n = x.shape[-1]
    lane = lax.broadcasted_iota(jnp.int32, x.shape, 1) % HEAD_DIM
    partner = jnp.where(lane < half, pltpu.roll(x, n - half, 1), pltpu.roll(x, half, 1))
    return x * cos + partner * sin_signed


def _shift_halo(buf, halo, ts, first):
    @pl.when(first)
    def _():
        buf[0:halo, :] = jnp.zeros((halo, GROUP), F32)

    @pl.when(jnp.logical_not(first))
    def _():
        buf[0:halo, :] = buf[ts:ts + halo, :]


def _conv_chunk(hbuf, r0, rc, cw_ref, cb_ref, lg_ref, lb_ref):
    back = 8 * ((CONV_WIDTH - 1) // 8)
    acc = jnp.broadcast_to(cb_ref[...], (rc, GROUP))
    for r in range(8):
        lo = CONV_HALO + r0 - back - r
        hr = hbuf[lo:lo + back + rc, :]
        part = None
        for q in range(back // 8 + 1):
            d = 8 * q + r
            if d < CONV_WIDTH:
                j = CONV_WIDTH - 1 - d
                term = cw_ref[j:j + 1, :] * hr[back - 8 * q:back - 8 * q + rc]
                part = term if part is None else part + term
        acc = acc + part
    mu = jnp.mean(acc, axis=-1, keepdims=True)
    d = acc - mu
    var = jnp.mean(d * d, axis=-1, keepdims=True)
    return _silu(d * lax.rsqrt(var + 1e-5) * lg_ref[...] + lb_ref[...])


def _pool_chunk(ubuf, r0, rc, t0):
    lane_grp = lax.broadcasted_iota(jnp.int32, (rc, GROUP), 1) // (GROUP // len(POOL_WINDOWS))
    t1 = (t0 + r0 + 1 + lax.broadcasted_iota(jnp.int32, (rc, 1), 0)).astype(F32)
    run = ubuf[POOL_HALO + r0 - 8:POOL_HALO + r0 + rc, :]
    u0 = run[8:]
    totals = {}
    for dlt in range(1, 8):
        run = run + ubuf[POOL_HALO + r0 - 8 - dlt:POOL_HALO + r0 + rc - dlt, :]
        totals[dlt + 1] = run[8:]
    totals[16] = run[8:] + run[:rc]
    pooled = jnp.zeros((rc, GROUP), F32)
    for gi, w in enumerate(POOL_WINDOWS):
        pooled = jnp.where(lane_grp == gi, totals[w] / jnp.minimum(t1, float(w)) - u0, pooled)
    return pooled


def _split_dot(x, p):
    hi = x.astype(BF16)
    lo = (x - hi.astype(F32)).astype(BF16)
    return (jnp.dot(hi, p, preferred_element_type=F32) + jnp.dot(lo, p, preferred_element_type=F32))


def _retention_tile(q, k, v, g, state, dm_ref, qd_ref, kd_ref, cd_ref, bd_ref, gg_ref, gb_ref, y_ref):
    C = RET_CHUNK
    chunks = [slice(r, r + C) for r in range(0, q.shape[0], C)]
    head = _lane_head((1, GROUP), 1)
    nt = (((1,), (1,)), ((), ()))
    masks = [head == h for h in range(N_HEADS)]
    bd = bd_ref[...]
    p = (bd * (1.0 / HEAD_DIM)).astype(BF16)
    scores = [[lax.dot_general(jnp.where(mh, q[rows], jnp.zeros_like(q[rows])), k[rows], nt,
                               preferred_element_type=F32) for mh in masks] for rows in chunks]
    o = []
    for rows in chunks:
        o.append(jnp.dot(q[rows], state[...].astype(BF16), preferred_element_type=F32) * qd_ref[...])
        kdec = (k[rows].astype(F32) * kd_ref[...]).astype(BF16)
        upd = lax.dot_general(kdec, v[rows], (((0,), (0,)), ((), ())), preferred_element_type=F32)
        state[...] = state[...] * cd_ref[...] + bd * upd
    for c, rows in enumerate(chunks):
        for h, mh in enumerate(masks):
            oh = jnp.dot((scores[c][h] * dm_ref[h]).astype(BF16), v[rows], preferred_element_type=F32)
            o[c] = o[c] + jnp.where(mh, oh, 0.0)
    d = [oc - _split_dot(oc, p) for oc in o]
    var = [_split_dot(dc * dc, p) for dc in d]
    for c, rows in enumerate(chunks):
        on = d[c] * lax.rsqrt(var[c] + 1e-5) * gg_ref[...] + gb_ref[...]
        y_ref[rows, :] = (_silu(g[rows]) * on).astype(BF16)


def _in_proj_kernel(per_seq, x_ref, g_ref, w_ref, cr_ref, sr_ref, cm_ref, sm_ref,
                    cw_ref, cb_ref, lg_ref, lb_ref, wp_ref, pb_ref, ps_ref,
                    dm_ref, qd_ref, kd_ref, cd_ref, bd_ref, gg_ref, gb_ref,
                    ya_ref, yb_ref, mqt_ref, mk_ref, mvt_ref, km_ref, yd_ref,
                    hbuf, ubuf, pool_scr, state):
    G = GROUP
    ts = x_ref.shape[0]
    s = pl.program_id(0) % per_seq
    _shift_halo(hbuf, CONV_HALO, ts, s == 0)
    _shift_halo(ubuf, POOL_HALO, ts, s == 0)

    @pl.when(s == 0)
    def _():
        state[...] = jnp.zeros_like(state)

    hb = _rms(x_ref[...], g_ref[...]).astype(BF16)

    def proj(i, n=1):
        return jnp.dot(hb, w_ref[:, i * G:(i + n) * G], preferred_element_type=F32)

    uc = proj(0, 2)
    hbuf[CONV_HALO:, :] = uc[:, :G] * jax.nn.sigmoid(uc[:, G:])
    ubuf[POOL_HALO:, :] = proj(9)
    cr, sr = cr_ref[...], sr_ref[...]
    rq = _rotate_half(proj(2), cr, sr, HEAD_DIM // 2).astype(BF16)
    rk = (_rotate_half(proj(3), cr, sr, HEAD_DIM // 2) * (HEAD_DIM ** -0.5)).astype(BF16)
    rv = proj(4).astype(BF16)
    rg = proj(5)
    cm, sm = cm_ref[...], sm_ref[...]
    mq = _rotate_half(proj(6), cm, sm, ROT_DIM // 2)
    mk = _rotate_half(proj(7), cm, sm, ROT_DIM // 2)
    mv = proj(8)
    mk_ref[...] = mk.astype(BF16)
    nblk = mq.shape[0] // MOBA_BLOCK
    for j in range(nblk):
        rows = slice(j * MOBA_BLOCK, (j + 1) * MOBA_BLOCK)
        mqt_ref[j] = mq[rows].T
        vt = mv[rows].T.astype(BF16)
        for h in range(N_HEADS):
            r = h * VT_ROWS
            mvt_ref[j, r:r + HEAD_DIM, :] = vt[h * HEAD_DIM:(h + 1) * HEAD_DIM]
            mvt_ref[j, r + HEAD_DIM:r + VT_ROWS, :] = jnp.ones((VT_ROWS - HEAD_DIM, MOBA_BLOCK), BF16)
        km_ref[0, j:j + 1, :] = jnp.mean(mk[rows], axis=0, keepdims=True)

    _retention_tile(rq, rk, rv, rg, state, dm_ref, qd_ref, kd_ref, cd_ref, bd_ref, gg_ref, gb_ref, yb_ref)

    rc = LOCAL_CHUNK
    for r0 in range(0, ts, rc):
        ya_ref[r0:r0 + rc, :] = _conv_chunk(hbuf, r0, rc, cw_ref, cb_ref, lg_ref, lb_ref).astype(BF16)
        pool_scr[r0:r0 + rc, :] = _pool_chunk(ubuf, r0, rc, s * ts).astype(BF16)
    y = jnp.dot(pool_scr[...], wp_ref[...], preferred_element_type=F32)
    yd_ref[...] = ((y + pb_ref[...]) * ps_ref[...]).astype(BF16)


def _in_proj(x2, layer, g, w_in, tabs, local, rtabs, gn, seq):
    T, D = x2.shape
    tm = ROW_TILE
    nt = T // tm
    per_seq = seq // tm
    nblk = tm // MOBA_BLOCK
    G = GROUP
    row = lambda i: (i, 0)
    tab = pl.BlockSpec((tm, G), lambda i: (i % per_seq, 0))
    lay = lambda a: _layer_spec(a, layer)
    bfo = jax.ShapeDtypeStruct((T, G), BF16)
    blk3 = lambda rows, dt: jax.ShapeDtypeStruct((T // MOBA_BLOCK, rows, MOBA_BLOCK), dt)
    o_row = pl.BlockSpec((tm, G), row)
    o_blk = lambda rows: pl.BlockSpec((nblk, rows, MOBA_BLOCK), lambda i: (i, 0, 0))
    return pl.pallas_call(
        functools.partial(_in_proj_kernel, per_seq),
        grid=(nt,),
        in_specs=[pl.BlockSpec((tm, D), row), lay(g), lay(w_in), tab, tab, tab, tab,
                  *[lay(a) for a in local],
                  *[_const_spec(a.shape) for a in rtabs],
                  *[lay(a) for a in gn]],
        out_specs=[o_row, o_row, o_blk(G), o_row, o_blk(N_HEADS * VT_ROWS),
                   pl.BlockSpec((1, nblk, G), lambda i: (i, 0, 0)), o_row],
        out_shape=[bfo, bfo, blk3(G, F32), bfo, blk3(N_HEADS * VT_ROWS, BF16),
                   jax.ShapeDtypeStruct((nt, nblk, G), F32), bfo],
        scratch_shapes=[pltpu.VMEM((CONV_HALO + tm, G), F32), pltpu.VMEM((POOL_HALO + tm, G), F32),
                        pltpu.VMEM((tm, G), BF16), pltpu.VMEM((G, G), F32)],
        compiler_params=_cparams(("arbitrary",)),
        name="in_proj",
    )(x2, g, w_in, *tabs, *local, *rtabs, *gn)


def _moba_kernel(qt_ref, k_ref, vt_ref, km_ref, y_ref, sel_ref, qh_ref, m_ref, acc_ref, ot_ref, sa_ref, sb_ref):
    a = pl.program_id(1)
    nb = km_ref.shape[1]
    Q = MOBA_BLOCK
    chains = [(t, h) for t in range(MOBA_TILES) for h in range(N_HEADS)]
    cid = lambda t, h: t * N_HEADS + h
    ch_head = _lane_head((GROUP, 1), 0)
    km = km_ref[0]
    km_head = _lane_head((1, GROUP), 1)
    blk = lax.broadcasted_iota(jnp.int32, (nb, Q), 0)

    km_heads = jnp.concatenate([jnp.where(km_head == h, km, 0.0) for h in range(N_HEADS)], axis=0)
    for t in range(MOBA_TILES):
        qt = qt_ref[t]
        qs = (qt * (HEAD_DIM ** -0.5 * LOG2E)).astype(BF16)
        gate_all = jnp.dot(km_heads, qt, preferred_element_type=F32, precision=lax.Precision.HIGHEST)
        past = blk < MOBA_TILES * a + t
        for h in range(N_HEADS):
            gate = jnp.where(past, gate_all[h * nb:(h + 1) * nb], -jnp.inf)
            sel = jnp.zeros((nb, Q), F32)
            for _ in range(MOBA_TOPK):
                top = jnp.max(gate, axis=0, keepdims=True)
                first = jnp.min(jnp.where(gate == top, blk, nb), axis=0, keepdims=True)
                pick = (blk == first) & past
                sel = jnp.where(pick, 1.0, sel)
                gate = jnp.where(pick, -jnp.inf, gate)
            sel_ref[cid(t, h) * nb:(cid(t, h) + 1) * nb, :] = sel
            qh_ref[cid(t, h)] = jnp.where(ch_head == h, qs, jnp.zeros_like(qs))

    m_ref[...] = jnp.full(m_ref.shape, -BIG, F32)
    acc_ref[...] = jnp.zeros(acc_ref.shape, F32)

    def score(j, s_ref, which):
        kb = k_ref[j]
        for t, h in which:
            s_ref[cid(t, h)] = jnp.dot(kb, qh_ref[cid(t, h)], preferred_element_type=F32)

    def absorb(j, s_ref, which, own_tile=None):
        causal = None
        for t, h in which:
            c = cid(t, h)
            m = m_ref[c:c + 1, :]
            s = s_ref[c]
            if t == own_tile:
                if causal is None:
                    causal = (lax.broadcasted_iota(jnp.int32, (Q, Q), 0)
                              <= lax.broadcasted_iota(jnp.int32, (Q, Q), 1))
                s = jnp.where(causal, s, NEG)
                m_new = jnp.maximum(m, jnp.max(s, axis=0, keepdims=True))
                shift = m_new
            else:
                on = sel_ref[pl.ds(c * nb + j, 1), :] > 0.0
                m_new = jnp.where(on, jnp.maximum(m, jnp.max(s, axis=0, keepdims=True)), m)
                shift = jnp.where(on, m_new, BIG)
            p = jnp.exp2(s - shift).astype(BF16)
            vt = vt_ref[j, h * VT_ROWS:(h + 1) * VT_ROWS, :]
            acc_ref[c] = jnp.exp2(m - m_new) * acc_ref[c] + jnp.dot(vt, p, preferred_element_type=F32)
            m_ref[c:c + 1, :] = m_new

    def body(i, carry):
        score(2 * i + 1, sb_ref, chains)
        absorb(2 * i, sa_ref, chains)
        score(2 * i + 2, sa_ref, chains)
        absorb(2 * i + 1, sb_ref, chains)
        return carry

    first = MOBA_TILES * a
    score(0, sa_ref, chains)
    lax.fori_loop(0, first // 2, body, 0)
    bufs = (sa_ref, sb_ref)
    for k in range(MOBA_TILES):
        if k + 1 < MOBA_TILES:
            score(first + k + 1, bufs[(k + 1) % 2], [c for c in chains if c[0] > k])
        absorb(first + k, bufs[k % 2], [c for c in chains if c[0] >= k], own_tile=k)

    for t in range(MOBA_TILES):
        for h in range(N_HEADS):
            acc = acc_ref[cid(t, h)]
            ot_ref[h * HEAD_DIM:(h + 1) * HEAD_DIM, :] = acc[:HEAD_DIM] / acc[HEAD_DIM:HEAD_DIM + 1]
        y_ref[t * Q:(t + 1) * Q, :] = ot_ref[...].T.astype(BF16)


def _moba(mqt, mk3, mvt, km, batch, seq):
    nb = seq // MOBA_BLOCK
    B = MOBA_BLOCK
    G = GROUP
    nt = MOBA_TILES
    assert nt % 2 == 0 and nb % nt == 0
    steps = nb // nt
    nc = nt * N_HEADS
    return pl.pallas_call(
        _moba_kernel,
        grid=(batch, steps),
        in_specs=[pl.BlockSpec((nt, G, B), lambda b, i: (b * steps + i, 0, 0)),
                  pl.BlockSpec((nb, B, G), lambda b, i: (b, 0, 0)),
                  pl.BlockSpec((nb, N_HEADS * VT_ROWS, B), lambda b, i: (b, 0, 0)),
                  pl.BlockSpec((1, nb, G), lambda b, i: (b, 0, 0))],
        out_specs=pl.BlockSpec((nt * B, G), lambda b, i: (b * steps + i, 0)),
        out_shape=jax.ShapeDtypeStruct((batch * seq, G), BF16),
        scratch_shapes=[pltpu.VMEM((nc * nb, B), F32),
                        pltpu.VMEM((nc, G, B), BF16),
                        pltpu.VMEM((nc, B), F32),
                        pltpu.VMEM((nc, VT_ROWS, B), F32),
                        pltpu.VMEM((G, B), F32),
                        pltpu.VMEM((nc, B, B), F32),
                        pltpu.VMEM((nc, B, B), F32)],
        compiler_params=_cparams(("arbitrary", "arbitrary")),
        name="moba",
    )(mqt, mk3, mvt, km)


def _post_kernel(x_ref, ya_ref, yb_ref, yc_ref, yd_ref, wo_ref, gpo_ref, gfi_ref,
                 wg_ref, wu_ref, wd_ref, gfo_ref, o_ref, f_scr):
    G = GROUP
    tm = x_ref.shape[0]
    d_ff = wg_ref.shape[1]
    halves = [slice(r, r + tm // POST_SPLIT) for r in range(0, tm, tm // POST_SPLIT)]
    ys = []
    for rows in halves:
        y = jnp.dot(ya_ref[rows, :], wo_ref[0:G, :], preferred_element_type=F32)
        y = y + jnp.dot(yb_ref[rows, :], wo_ref[G:2 * G, :], preferred_element_type=F32)
        y = y + jnp.dot(yc_ref[rows, :], wo_ref[2 * G:3 * G, :], preferred_element_type=F32)
        ys.append(y + jnp.dot(yd_ref[rows, :], wo_ref[3 * G:4 * G, :], preferred_element_type=F32))
    for rows, y in zip(halves, ys):
        x1 = x_ref[rows, :] + _rms(y, gpo_ref[...])
        hb = _rms(x1, gfi_ref[...]).astype(BF16)
        for c in range(0, d_ff, FF_CHUNK):
            cols = slice(c, c + FF_CHUNK)
            gt = jnp.dot(hb, wg_ref[:, cols], preferred_element_type=F32)
            up = jnp.dot(hb, wu_ref[:, cols], preferred_element_type=F32)
            f_scr[rows, cols] = (_silu(gt) * up).astype(BF16)
        f = jnp.dot(f_scr[rows, :], wd_ref[...], preferred_element_type=F32)
        o_ref[rows, :] = x1 + _rms(f, gfo_ref[...])


def _post(x2, layer, ya, yb, yc, yd, params):
    T, D = x2.shape
    tm = POST_TILE
    G = GROUP
    d_ff = params[3].shape[-1]
    row = lambda i: (i, 0)
    xs = pl.BlockSpec((tm, D), row)
    ys = pl.BlockSpec((tm, G), row)
    return pl.pallas_call(
        _post_kernel,
        grid=(T // tm,),
        in_specs=[xs, ys, ys, ys, ys, *[_layer_spec(a, layer) for a in params]],
        out_specs=xs,
        out_shape=jax.ShapeDtypeStruct((T, D), F32),
        scratch_shapes=[pltpu.VMEM((tm, d_ff), BF16)],
        compiler_params=_cparams(("arbitrary",)),
        name="out_proj_ffn",
    )(x2, ya, yb, yc, yd, *params)


def _rotary_tables(seq, rot_dim, theta):
    pos = jnp.arange(seq, dtype=F32)
    inv = 1.0 / (theta ** (jnp.arange(0, rot_dim, 2, dtype=F32) / rot_dim))
    ang = pos[:, None] * inv[None, :]
    cos, sin = jnp.cos(ang), jnp.sin(ang)
    rest = HEAD_DIM - rot_dim
    cos_h = jnp.concatenate([cos, cos, jnp.ones((seq, rest), F32)], axis=-1)
    sin_h = jnp.concatenate([-sin, sin, jnp.zeros((seq, rest), F32)], axis=-1)
    return jnp.tile(cos_h, (1, N_HEADS)), jnp.tile(sin_h, (1, N_HEADS))


def _retention_tables():
    c = RET_CHUNK
    log_gamma = jnp.log(1.0 - 2.0 ** (-5.0 - jnp.arange(N_HEADS, dtype=F32)))
    idx = jnp.arange(c, dtype=F32)
    diff = idx[:, None] - idx[None, :]
    dm = jnp.where(diff[None] >= 0, jnp.exp(jnp.maximum(diff, 0.0)[None] * log_gamma[:, None, None]), 0.0)
    lanes = lambda t: jnp.repeat(t, HEAD_DIM, axis=-1)
    qd = lanes(jnp.exp((idx + 1.0)[:, None] * log_gamma[None, :]))
    kd = lanes(jnp.exp((c - 1.0 - idx)[:, None] * log_gamma[None, :]))
    cd = lanes(jnp.exp(c * log_gamma)[None, :])
    head = jnp.arange(GROUP) // HEAD_DIM
    bd = (head[:, None] == head[None, :]).astype(F32)
    return dm, qd, kd, cd, bd


def _block_diag(w):
    n, c, _ = w.shape
    eye = jnp.eye(n, dtype=w.dtype)
    return (eye[:, None, :, None] * w[:, :, None, :]).reshape(n * c, n * c)


def kernel(x, attn_pre_g, w_in, conv_w, conv_b, conv_ln_g, conv_ln_b, ret_gn_g, ret_gn_b, pool_w, pool_b,
           pool_scale, w_out, attn_post_g, ffn_pre_g, w_gate, w_up, w_down, ffn_post_g):
    batch, seq, d_model = x.shape
    depth = w_in.shape[0]
    assert seq % ROW_TILE == 0 and ROW_TILE % LOCAL_CHUNK == 0 and seq % RET_CHUNK == 0 and seq % MOBA_BLOCK == 0
    assert (batch * seq) % POST_TILE == 0
    nb = seq // MOBA_BLOCK
    tabs = _rotary_tables(seq, HEAD_DIM, RET_ROT_THETA) + _rotary_tables(seq, ROT_DIM, ROPE_THETA)
    rtabs = _retention_tables()
    vec = lambda a: a.reshape(depth, 1, -1)
    bf = lambda a: a.astype(BF16)
    in_g, in_w = vec(attn_pre_g), bf(w_in)
    local = (conv_w, vec(conv_b), vec(conv_ln_g), vec(conv_ln_b),
             bf(jax.vmap(_block_diag)(pool_w)), vec(pool_b), vec(pool_scale))
    gn = (vec(ret_gn_g), vec(ret_gn_b))
    post = (bf(w_out), vec(attn_post_g), vec(ffn_pre_g), bf(w_gate), bf(w_up), bf(w_down), vec(ffn_post_g))

    x2 = x.reshape(batch * seq, d_model)
    for l in range(depth):
        ya, yb, mqt, mk, mvt, km, yd = _in_proj(x2, l, in_g, in_w, tabs, local, rtabs, gn, seq)
        yc = _moba(mqt, mk.reshape(batch * nb, MOBA_BLOCK, GROUP), mvt, km.reshape(batch, nb, GROUP), batch, seq)
        x2 = _post(x2, l, ya, yb, yc, yd, post)
    return x2.reshape(batch, seq, d_model)
```

```python
import functools
import math

import jax
import jax.numpy as jnp
from jax import lax
from jax.experimental import pallas as pl
from jax.experimental.pallas import tpu as pltpu

F32 = jnp.float32
BF16 = jnp.bfloat16

GROUP = 256
HEAD_DIM = 64
N_HEADS = GROUP // HEAD_DIM
CONV_WIDTH = 31
RET_ROT_THETA = 10000.0
MOBA_BLOCK = 256
MOBA_TOPK = 3
ROPE_THETA = 500000.0
ROT_DIM = HEAD_DIM // 4
POOL_WINDOWS = (2, 4, 8, 16)
NEG = -1e30
BIG = 1e30
VT_ROWS = HEAD_DIM + 16
LOG2E = math.log2(math.e)

ROW_TILE = 1024
POST_TILE = 1024
POST_SPLIT = 4
LOCAL_CHUNK = 128
CONV_HALO = 32
POOL_HALO = 16
RET_CHUNK = 256
FF_CHUNK = 256
MOBA_TILES = 2
VMEM_LIMIT = 56 * 1024 * 1024


def _cparams(sem):
    return pltpu.CompilerParams(dimension_semantics=sem, vmem_limit_bytes=VMEM_LIMIT)


def _const_spec(shape):
    nd = len(shape)
    return pl.BlockSpec(shape, lambda *_: (0,) * nd, pipeline_mode=pl.Buffered(1))


def _layer_spec(stacked, layer):
    shape = stacked.shape[1:]
    nd = len(shape)
    return pl.BlockSpec((None,) + shape, lambda *_: (layer,) + (0,) * nd, pipeline_mode=pl.Buffered(1))


def _rms(x, g):
    return x * lax.rsqrt(jnp.mean(x * x, axis=-1, keepdims=True) + 1e-6) * g


def _silu(x):
    return x * jax.nn.sigmoid(x)


def _lane_head(shape, axis):
    return lax.broadcasted_iota(jnp.int32, shape, axis) // HEAD_DIM


def _rotate_half(x, cos, sin_signed, half):
    n = x.shape[-1]
    lane = lax.broadcasted_iota(jnp.int32, x.shape, 1) % HEAD_DIM
    partner = jnp.where(lane < half, pltpu.roll(x, n - half, 1), pltpu.roll(x, half, 1))
    return x * cos + partner * sin_signed


def _shift_halo(buf, halo, ts, first):
    @pl.when(first)
    def _():
        buf[0:halo, :] = jnp.zeros((halo, GROUP), F32)

    @pl.when(jnp.logical_not(first))
    def _():
        buf[0:halo, :] = buf[ts:ts + halo, :]


def _conv_chunk(hbuf, r0, rc, cw_ref, cb_ref, lg_ref, lb_ref):
    back = 8 * ((CONV_WIDTH - 1) // 8)
    acc = jnp.broadcast_to(cb_ref[...], (rc, GROUP))
    for r in range(8):
        lo = CONV_HALO + r0 - back - r
        hr = hbuf[lo:lo + back + rc, :]
        part = None
        for q in range(back // 8 + 1):
            d = 8 * q + r
            if d < CONV_WIDTH:
                j = CONV_WIDTH - 1 - d
                term = cw_ref[j:j + 1, :] * hr[back - 8 * q:back - 8 * q + rc]
                part = term if part is None else part + term
        acc = acc + part
    mu = jnp.mean(acc, axis=-1, keepdims=True)
    d = acc - mu
    var = jnp.mean(d * d, axis=-1, keepdims=True)
    return _silu(d * lax.rsqrt(var + 1e-5) * lg_ref[...] + lb_ref[...])


def _pool_chunk(ubuf, r0, rc, t0):
    lane_grp = lax.broadcasted_iota(jnp.int32, (rc, GROUP), 1) // (GROUP // len(POOL_WINDOWS))
    t1 = (t0 + r0 + 1 + lax.broadcasted_iota(jnp.int32, (rc, 1), 0)).astype(F32)
    run = ubuf[POOL_HALO + r0 - 8:POOL_HALO + r0 + rc, :]
    u0 = run[8:]
    totals = {}
    for dlt in range(1, 8):
        run = run + ubuf[POOL_HALO + r0 - 8 - dlt:POOL_HALO + r0 + rc - dlt, :]
        totals[dlt + 1] = run[8:]
    totals[16] = run[8:] + run[:rc]
    pooled = jnp.zeros((rc, GROUP), F32)
    for gi, w in enumerate(POOL_WINDOWS):
        pooled = jnp.where(lane_grp == gi, totals[w] / jnp.minimum(t1, float(w)) - u0, pooled)
    return pooled


def _split_dot(x, p):
    hi = x.astype(BF16)
    lo = (x - hi.astype(F32)).astype(BF16)
    return (jnp.dot(hi, p, preferred_element_type=F32) + jnp.dot(lo, p, preferred_element_type=F32))


def _retention_tile(q, k, v, g, state, dm_ref, qd_ref, kd_ref, cd_ref, bd_ref, gg_ref, gb_ref, y_ref):
    C = RET_CHUNK
    chunks = [slice(r, r + C) for r in range(0, q.shape[0], C)]
    head = _lane_head((1, GROUP), 1)
    nt = (((1,), (1,)), ((), ()))
    masks = [head == h for h in range(N_HEADS)]
    bd = bd_ref[...]
    p = (bd * (1.0 / HEAD_DIM)).astype(BF16)
    scores = [[lax.dot_general(jnp.where(mh, q[rows], jnp.zeros_like(q[rows])), k[rows], nt,
                               preferred_element_type=F32) for mh in masks] for rows in chunks]
    o = []
    for rows in chunks:
        o.append(jnp.dot(q[rows], state[...].astype(BF16), preferred_element_type=F32) * qd_ref[...])
        kdec = (k[rows].astype(F32) * kd_ref[...]).astype(BF16)
        upd = lax.dot_general(kdec, v[rows], (((0,), (0,)), ((), ())), preferred_element_type=F32)
        state[...] = state[...] * cd_ref[...] + bd * upd
    for c, rows in enumerate(chunks):
        for h, mh in enumerate(masks):
            oh = jnp.dot((scores[c][h] * dm_ref[h]).astype(BF16), v[rows], preferred_element_type=F32)
            o[c] = o[c] + jnp.where(mh, oh, 0.0)
    d = [oc - _split_dot(oc, p) for oc in o]
    var = [_split_dot(dc * dc, p) for dc in d]
    for c, rows in enumerate(chunks):
        on = d[c] * lax.rsqrt(var[c] + 1e-5) * gg_ref[...] + gb_ref[...]
        y_ref[rows, :] = (_silu(g[rows]) * on).astype(BF16)


def _in_proj_kernel(per_seq, x_ref, g_ref, w_ref, cr_ref, sr_ref, cm_ref, sm_ref,
                    cw_ref, cb_ref, lg_ref, lb_ref, wp_ref, pb_ref, ps_ref,
                    dm_ref, qd_ref, kd_ref, cd_ref, bd_ref, gg_ref, gb_ref,
                    ya_ref, yb_ref, mqt_ref, mk_ref, mvt_ref, km_ref, yd_ref,
                    hbuf, ubuf, pool_scr, state):
    G = GROUP
    ts = x_ref.shape[0]
    s = pl.program_id(0) % per_seq
    _shift_halo(hbuf, CONV_HALO, ts, s == 0)
    _shift_halo(ubuf, POOL_HALO, ts, s == 0)

    @pl.when(s == 0)
    def _():
        state[...] = jnp.zeros_like(state)

    hb = _rms(x_ref[...], g_ref[...]).astype(BF16)

    def proj(i, n=1):
        return jnp.dot(hb, w_ref[:, i * G:(i + n) * G], preferred_element_type=F32)

    uc = proj(0, 2)
    hbuf[CONV_HALO:, :] = uc[:, :G] * jax.nn.sigmoid(uc[:, G:])
    ubuf[POOL_HALO:, :] = proj(9)
    cr, sr = cr_ref[...], sr_ref[...]
    rq = _rotate_half(proj(2), cr, sr, HEAD_DIM // 2).astype(BF16)
    rk = (_rotate_half(proj(3), cr, sr, HEAD_DIM // 2) * (HEAD_DIM ** -0.5)).astype(BF16)
    rv = proj(4).astype(BF16)
    rg = proj(5)
    cm, sm = cm_ref[...], sm_ref[...]
    mq = _rotate_half(proj(6), cm, sm, ROT_DIM // 2)
    mk = _rotate_half(proj(7), cm, sm, ROT_DIM // 2)
    mv = proj(8)
    mk_ref[...] = mk.astype(BF16)
    nblk = mq.shape[0] // MOBA_BLOCK
    for j in range(nblk):
        rows = slice(j * MOBA_BLOCK, (j + 1) * MOBA_BLOCK)
        mqt_ref[j] = mq[rows].T
        vt = mv[rows].T.astype(BF16)
        for h in range(N_HEADS):
            r = h * VT_ROWS
            mvt_ref[j, r:r + HEAD_DIM, :] = vt[h * HEAD_DIM:(h + 1) * HEAD_DIM]
            mvt_ref[j, r + HEAD_DIM:r + VT_ROWS, :] = jnp.ones((VT_ROWS - HEAD_DIM, MOBA_BLOCK), BF16)
        km_ref[0, j:j + 1, :] = jnp.mean(mk[rows], axis=0, keepdims=True)

    _retention_tile(rq, rk, rv, rg, state, dm_ref, qd_ref, kd_ref, cd_ref, bd_ref, gg_ref, gb_ref, yb_ref)

    rc = LOCAL_CHUNK
    for r0 in range(0, ts, rc):
        ya_ref[r0:r0 + rc, :] = _conv_chunk(hbuf, r0, rc, cw_ref, cb_ref, lg_ref, lb_ref).astype(BF16)
        pool_scr[r0:r0 + rc, :] = _pool_chunk(ubuf, r0, rc, s * ts).astype(BF16)
    y = jnp.dot(pool_scr[...], wp_ref[...], preferred_element_type=F32)
    yd_ref[...] = ((y + pb_ref[...]) * ps_ref[...]).astype(BF16)


def _in_proj(x2, layer, g, w_in, tabs, local, rtabs, gn, seq):
    T, D = x2.shape
    tm = ROW_TILE
    nt = T // tm
    per_seq = seq // tm
    nblk = tm // MOBA_BLOCK
    G = GROUP
    row = lambda i: (i, 0)
    tab = pl.BlockSpec((tm, G), lambda i: (i % per_seq, 0))
    lay = lambda a: _layer_spec(a, layer)
    bfo = jax.ShapeDtypeStruct((T, G), BF16)
    blk3 = lambda rows, dt: jax.ShapeDtypeStruct((T // MOBA_BLOCK, rows, MOBA_BLOCK), dt)
    o_row = pl.BlockSpec((tm, G), row)
    o_blk = lambda rows: pl.BlockSpec((nblk, rows, MOBA_BLOCK), lambda i: (i, 0, 0))
    return pl.pallas_call(
        functools.partial(_in_proj_kernel, per_seq),
        grid=(nt,),
        in_specs=[pl.BlockSpec((tm, D), row), lay(g), lay(w_in), tab, tab, tab, tab,
                  *[lay(a) for a in local],
                  *[_const_spec(a.shape) for a in rtabs],
                  *[lay(a) for a in gn]],
        out_specs=[o_row, o_row, o_blk(G), o_row, o_blk(N_HEADS * VT_ROWS),
                   pl.BlockSpec((1, nblk, G), lambda i: (i, 0, 0)), o_row],
        out_shape=[bfo, bfo, blk3(G, F32), bfo, blk3(N_HEADS * VT_ROWS, BF16),
                   jax.ShapeDtypeStruct((nt, nblk, G), F32), bfo],
        scratch_shapes=[pltpu.VMEM((CONV_HALO + tm, G), F32), pltpu.VMEM((POOL_HALO + tm, G), F32),
                        pltpu.VMEM((tm, G), BF16), pltpu.VMEM((G, G), F32)],
        compiler_params=_cparams(("arbitrary",)),
        name="in_proj",
    )(x2, g, w_in, *tabs, *local, *rtabs, *gn)


def _dot3(a, b):
    ah, bh = a.astype(BF16), b.astype(BF16)
    al, bl = (a - ah.astype(F32)).astype(BF16), (b - bh.astype(F32)).astype(BF16)
    dot = functools.partial(jnp.dot, preferred_element_type=F32)
    return dot(ah, bh) + (dot(ah, bl) + dot(al, bh))


def _moba_kernel(qt_ref, k_ref, vt_ref, km_ref, y_ref, sel_ref, qh_ref, m_ref, acc_ref, ot_ref, sa_ref, sb_ref):
    a = pl.program_id(1)
    nb = km_ref.shape[1]
    Q = MOBA_BLOCK
    chains = [(t, h) for t in range(MOBA_TILES) for h in range(N_HEADS)]
    cid = lambda t, h: t * N_HEADS + h
    ch_head = _lane_head((GROUP, 1), 0)
    km = km_ref[0]
    km_head = _lane_head((1, GROUP), 1)
    blk = lax.broadcasted_iota(jnp.int32, (nb, Q), 0)

    km_heads = jnp.concatenate([jnp.where(km_head == h, km, 0.0) for h in range(N_HEADS)], axis=0)
    for t in range(MOBA_TILES):
        qt = qt_ref[t]
        qs = (qt * (HEAD_DIM ** -0.5 * LOG2E)).astype(BF16)
        gate_all = _dot3(km_heads, qt)
        past = blk < MOBA_TILES * a + t
        for h in range(N_HEADS):
            gate = jnp.where(past, gate_all[h * nb:(h + 1) * nb], -jnp.inf)
            sel = jnp.zeros((nb, Q), F32)
            for _ in range(MOBA_TOPK):
                top = jnp.max(gate, axis=0, keepdims=True)
                first = jnp.min(jnp.where(gate == top, blk, nb), axis=0, keepdims=True)
                pick = (blk == first) & past
                sel = jnp.where(pick, 1.0, sel)
                gate = jnp.where(pick, -jnp.inf, gate)
            sel_ref[cid(t, h) * nb:(cid(t, h) + 1) * nb, :] = sel
            qh_ref[cid(t, h)] = jnp.where(ch_head == h, qs, jnp.zeros_like(qs))

    m_ref[...] = jnp.full(m_ref.shape, -BIG, F32)
    acc_ref[...] = jnp.zeros(acc_ref.shape, F32)

    def score(j, s_ref, which):
        kb = k_ref[j]
        for t, h in which:
            s_ref[cid(t, h)] = jnp.dot(kb, qh_ref[cid(t, h)], preferred_element_type=F32)

    def absorb(j, s_ref, which, own_tile=None):
        causal = None
        for t, h in which:
            c = cid(t, h)
            m = m_ref[c:c + 1, :]
            s = s_ref[c]
            if t == own_tile:
                if causal is None:
                    causal = (lax.broadcasted_iota(jnp.int32, (Q, Q), 0)
                              <= lax.broadcasted_iota(jnp.int32, (Q, Q), 1))
                s = jnp.where(causal, s, NEG)
                m_new = jnp.maximum(m, jnp.max(s, axis=0, keepdims=True))
                shift = m_new
            else:
                on = sel_ref[pl.ds(c * nb + j, 1), :] > 0.0
                m_new = jnp.where(on, jnp.maximum(m, jnp.max(s, axis=0, keepdims=True)), m)
                shift = jnp.where(on, m_new, BIG)
            p = jnp.exp2(s - shift).astype(BF16)
            vt = vt_ref[j, h * VT_ROWS:(h + 1) * VT_ROWS, :]
            acc_ref[c] = jnp.exp2(m - m_new) * acc_ref[c] + jnp.dot(vt, p, preferred_element_type=F32)
            m_ref[c:c + 1, :] = m_new

    def body(i, carry):
        score(2 * i + 1, sb_ref, chains)
        absorb(2 * i, sa_ref, chains)
        score(2 * i + 2, sa_ref, chains)
        absorb(2 * i + 1, sb_ref, chains)
        return carry

    first = MOBA_TILES * a
    score(0, sa_ref, chains)
    lax.fori_loop(0, first // 2, body, 0)
    bufs = (sa_ref, sb_ref)
    for k in range(MOBA_TILES):
        if k + 1 < MOBA_TILES:
            score(first + k + 1, bufs[(k + 1) % 2], [c for c in chains if c[0] > k])
        absorb(first + k, bufs[k % 2], [c for c in chains if c[0] >= k], own_tile=k)

    for t in range(MOBA_TILES):
        for h in range(N_HEADS):
            acc = acc_ref[cid(t, h)]
            ot_ref[h * HEAD_DIM:(h + 1) * HEAD_DIM, :] = acc[:HEAD_DIM] / acc[HEAD_DIM:HEAD_DIM + 1]
        y_ref[t * Q:(t + 1) * Q, :] = ot_ref[...].T.astype(BF16)


def _moba(mqt, mk3, mvt, km, batch, seq):
    nb = seq // MOBA_BLOCK
    B = MOBA_BLOCK
    G = GROUP
    nt = MOBA_TILES
    assert nt % 2 == 0 and nb % nt == 0
    steps = nb // nt
    nc = nt * N_HEADS
    return pl.pallas_call(
        _moba_kernel,
        grid=(batch, steps),
        in_specs=[pl.BlockSpec((nt, G, B), lambda b, i: (b * steps + i, 0, 0)),
                  pl.BlockSpec((nb, B, G), lambda b, i: (b, 0, 0)),
                  pl.BlockSpec((nb, N_HEADS * VT_ROWS, B), lambda b, i: (b, 0, 0)),
                  pl.BlockSpec((1, nb, G), lambda b, i: (b, 0, 0))],
        out_specs=pl.BlockSpec((nt * B, G), lambda b, i: (b * steps + i, 0)),
        out_shape=jax.ShapeDtypeStruct((batch * seq, G), BF16),
        scratch_shapes=[pltpu.VMEM((nc * nb, B), F32),
                        pltpu.VMEM((nc, G, B), BF16),
                        pltpu.VMEM((nc, B), F32),
                        pltpu.VMEM((nc, VT_ROWS, B), F32),
                        pltpu.VMEM((G, B), F32),
                        pltpu.VMEM((nc, B, B), F32),
                        pltpu.VMEM((nc, B, B), F32)],
        compiler_params=_cparams(("arbitrary", "arbitrary")),
        name="moba",
    )(mqt, mk3, mvt, km)


def _post_kernel(x_ref, ya_ref, yb_ref, yc_ref, yd_ref, wo_ref, gpo_ref, gfi_ref,
                 wg_ref, wu_ref, wd_ref, gfo_ref, o_ref, f_scr):
    G = GROUP
    tm = x_ref.shape[0]
    d_ff = wg_ref.shape[1]
    halves = [slice(r, r + tm // POST_SPLIT) for r in range(0, tm, tm // POST_SPLIT)]
    ys = []
    for rows in halves:
        y = jnp.dot(ya_ref[rows, :], wo_ref[0:G, :], preferred_element_type=F32)
        y = y + jnp.dot(yb_ref[rows, :], wo_ref[G:2 * G, :], preferred_element_type=F32)
        y = y + jnp.dot(yc_ref[rows, :], wo_ref[2 * G:3 * G, :], preferred_element_type=F32)
        ys.append(y + jnp.dot(yd_ref[rows, :], wo_ref[3 * G:4 * G, :], preferred_element_type=F32))
    for rows, y in zip(halves, ys):
        x1 = x_ref[rows, :] + _rms(y, gpo_ref[...])
        hb = _rms(x1, gfi_ref[...]).astype(BF16)
        for c in range(0, d_ff, FF_CHUNK):
            cols = slice(c, c + FF_CHUNK)
            gt = jnp.dot(hb, wg_ref[:, cols], preferred_element_type=F32)
            up = jnp.dot(hb, wu_ref[:, cols], preferred_element_type=F32)
            f_scr[rows, cols] = (_silu(gt) * up).astype(BF16)
        f = jnp.dot(f_scr[rows, :], wd_ref[...], preferred_element_type=F32)
        o_ref[rows, :] = x1 + _rms(f, gfo_ref[...])


def _post(x2, layer, ya, yb, yc, yd, params):
    T, D = x2.shape
    tm = POST_TILE
    G = GROUP
    d_ff = params[3].shape[-1]
    row = lambda i: (i, 0)
    xs = pl.BlockSpec((tm, D), row)
    ys = pl.BlockSpec((tm, G), row)
    return pl.pallas_call(
        _post_kernel,
        grid=(T // tm,),
        in_specs=[xs, ys, ys, ys, ys, *[_layer_spec(a, layer) for a in params]],
        out_specs=xs,
        out_shape=jax.ShapeDtypeStruct((T, D), F32),
        scratch_shapes=[pltpu.VMEM((tm, d_ff), BF16)],
        compiler_params=_cparams(("arbitrary",)),
        name="out_proj_ffn",
    )(x2, ya, yb, yc, yd, *params)


def _rotary_tables(seq, rot_dim, theta):
    pos = jnp.arange(seq, dtype=F32)
    inv = 1.0 / (theta ** (jnp.arange(0, rot_dim, 2, dtype=F32) / rot_dim))
    ang = pos[:, None] * inv[None, :]
    cos, sin = jnp.cos(ang), jnp.sin(ang)
    rest = HEAD_DIM - rot_dim
    cos_h = jnp.concatenate([cos, cos, jnp.ones((seq, rest), F32)], axis=-1)
    sin_h = jnp.concatenate([-sin, sin, jnp.zeros((seq, rest), F32)], axis=-1)
    return jnp.tile(cos_h, (1, N_HEADS)), jnp.tile(sin_h, (1, N_HEADS))


def _retention_tables():
    c = RET_CHUNK
    log_gamma = jnp.log(1.0 - 2.0 ** (-5.0 - jnp.arange(N_HEADS, dtype=F32)))
    idx = jnp.arange(c, dtype=F32)
    diff = idx[:, None] - idx[None, :]
    dm = jnp.where(diff[None] >= 0, jnp.exp(jnp.maximum(diff, 0.0)[None] * log_gamma[:, None, None]), 0.0)
    lanes = lambda t: jnp.repeat(t, HEAD_DIM, axis=-1)
    qd = lanes(jnp.exp((idx + 1.0)[:, None] * log_gamma[None, :]))
    kd = lanes(jnp.exp((c - 1.0 - idx)[:, None] * log_gamma[None, :]))
    cd = lanes(jnp.exp(c * log_gamma)[None, :])
    head = jnp.arange(GROUP) // HEAD_DIM
    bd = (head[:, None] == head[None, :]).astype(F32)
    return dm, qd, kd, cd, bd


def _block_diag(w):
    n, c, _ = w.shape
    eye = jnp.eye(n, dtype=w.dtype)
    return (eye[:, None, :, None] * w[:, :, None, :]).reshape(n * c, n * c)


def kernel(x, attn_pre_g, w_in, conv_w, conv_b, conv_ln_g, conv_ln_b, ret_gn_g, ret_gn_b, pool_w, pool_b,
           pool_scale, w_out, attn_post_g, ffn_pre_g, w_gate, w_up, w_down, ffn_post_g):
    batch, seq, d_model = x.shape
    depth = w_in.shape[0]
    assert seq % ROW_TILE == 0 and ROW_TILE % LOCAL_CHUNK == 0 and seq % RET_CHUNK == 0 and seq % MOBA_BLOCK == 0
    assert (batch * seq) % POST_TILE == 0
    nb = seq // MOBA_BLOCK
    tabs = _rotary_tables(seq, HEAD_DIM, RET_ROT_THETA) + _rotary_tables(seq, ROT_DIM, ROPE_THETA)
    rtabs = _retention_tables()
    vec = lambda a: a.reshape(depth, 1, -1)
    bf = lambda a: a.astype(BF16)
    in_g, in_w = vec(attn_pre_g), bf(w_in)
    local = (conv_w, vec(conv_b), vec(conv_ln_g), vec(conv_ln_b),
             bf(jax.vmap(_block_diag)(pool_w)), vec(pool_b), vec(pool_scale))
    gn = (vec(ret_gn_g), vec(ret_gn_b))
    post = (bf(w_out), vec(attn_post_g), vec(ffn_pre_g), bf(w_gate), bf(w_up), bf(w_down), vec(ffn_post_g))

    x2 = x.reshape(batch * seq, d_model)
    for l in range(depth):
        ya, yb, mqt, mk, mvt, km, yd = _in_proj(x2, l, in_g, in_w, tabs, local, rtabs, gn, seq)
        yc = _moba(mqt, mk.reshape(batch * nb, MOBA_BLOCK, GROUP), mvt, km.reshape(batch, nb, GROUP), batch, seq)
        x2 = _post(x2, l, ya, yb, yc, yd, post)
    return x2.reshape(batch, seq, d_model)
```

```python
import functools
import math

import jax
import jax.numpy as jnp
from jax import lax
from jax.experimental import pallas as pl
from jax.experimental.pallas import tpu as pltpu

F32 = jnp.float32
BF16 = jnp.bfloat16

GROUP = 256
HEAD_DIM = 64
N_HEADS = GROUP // HEAD_DIM
CONV_WIDTH = 31
RET_ROT_THETA = 10000.0
MOBA_BLOCK = 256
MOBA_TOPK = 3
ROPE_THETA = 500000.0
ROT_DIM = HEAD_DIM // 4
POOL_WINDOWS = (2, 4, 8, 16)
NEG = -1e30
BIG = 1e30
VT_ROWS = HEAD_DIM + 16
LOG2E = math.log2(math.e)

ROW_TILE = 1024
POST_TILE = 1024
POST_SPLIT = 4
LOCAL_CHUNK = 256
CONV_HALO = 32
POOL_HALO = 16
RET_CHUNK = 256
FF_CHUNK = 256
MOBA_TILES = 2
VMEM_LIMIT = 56 * 1024 * 1024


def _cparams(sem):
    return pltpu.CompilerParams(dimension_semantics=sem, vmem_limit_bytes=VMEM_LIMIT)


def _const_spec(shape):
    nd = len(shape)
    return pl.BlockSpec(shape, lambda *_: (0,) * nd, pipeline_mode=pl.Buffered(1))


def _layer_spec(stacked, layer):
    shape = stacked.shape[1:]
    nd = len(shape)
    return pl.BlockSpec((None,) + shape, lambda *_: (layer,) + (0,) * nd, pipeline_mode=pl.Buffered(1))


def _rms(x, g):
    return x * lax.rsqrt(jnp.mean(x * x, axis=-1, keepdims=True) + 1e-6) * g


def _silu(x):
    return x * jax.nn.sigmoid(x)


def _lane_head(shape, axis):
    return lax.broadcasted_iota(jnp.int32, shape, axis) // HEAD_DIM


def _rotate_half(x, cos, sin_signed, half):
    n = x.shape[-1]
    lane = lax.broadcasted_iota(jnp.int32, x.shape, 1) % HEAD_DIM
    partner = jnp.where(lane < half, pltpu.roll(x, n - half, 1), pltpu.roll(x, half, 1))
    return x * cos + partner * sin_signed


def _shift_halo(buf, halo, ts, first):
    @pl.when(first)
    def _():
        buf[0:halo, :] = jnp.zeros((halo, GROUP), F32)

    @pl.when(jnp.logical_not(first))
    def _():
        buf[0:halo, :] = buf[ts:ts + halo, :]


def _conv_chunk(hbuf, r0, rc, cw_ref, cb_ref, lg_ref, lb_ref):
    back = 8 * ((CONV_WIDTH - 1) // 8)
    acc = jnp.broadcast_to(cb_ref[...], (rc, GROUP))
    for r in range(8):
        lo = CONV_HALO + r0 - back - r
        hr = hbuf[lo:lo + back + rc, :]
        part = None
        for q in range(back // 8 + 1):
            d = 8 * q + r
            if d < CONV_WIDTH:
                j = CONV_WIDTH - 1 - d
                term = cw_ref[j:j + 1, :] * hr[back - 8 * q:back - 8 * q + rc]
                part = term if part is None else part + term
        acc = acc + part
    mu = jnp.mean(acc, axis=-1, keepdims=True)
    d = acc - mu
    var = jnp.mean(d * d, axis=-1, keepdims=True)
    return _silu(d * lax.rsqrt(var + 1e-5) * lg_ref[...] + lb_ref[...])


def _pool_chunk(ubuf, r0, rc, t0):
    lane_grp = lax.broadcasted_iota(jnp.int32, (rc, GROUP), 1) // (GROUP // len(POOL_WINDOWS))
    t1 = (t0 + r0 + 1 + lax.broadcasted_iota(jnp.int32, (rc, 1), 0)).astype(F32)
    run = ubuf[POOL_HALO + r0 - 8:POOL_HALO + r0 + rc, :]
    u0 = run[8:]
    totals = {}
    for dlt in range(1, 8):
        run = run + ubuf[POOL_HALO + r0 - 8 - dlt:POOL_HALO + r0 + rc - dlt, :]
        totals[dlt + 1] = run[8:]
    totals[16] = run[8:] + run[:rc]
    pooled = jnp.zeros((rc, GROUP), F32)
    for gi, w in enumerate(POOL_WINDOWS):
        pooled = jnp.where(lane_grp == gi, totals[w] / jnp.minimum(t1, float(w)) - u0, pooled)
    return pooled


def _split_dot(x, p):
    hi = x.astype(BF16)
    lo = (x - hi.astype(F32)).astype(BF16)
    return (jnp.dot(hi, p, preferred_element_type=F32) + jnp.dot(lo, p, preferred_element_type=F32))


def _retention_tile(q, k, v, g, state, dm_ref, qd_ref, kd_ref, cd_ref, bd_ref, gg_ref, gb_ref, y_ref):
    C = RET_CHUNK
    chunks = [slice(r, r + C) for r in range(0, q.shape[0], C)]
    head = _lane_head((1, GROUP), 1)
    nt = (((1,), (1,)), ((), ()))
    masks = [head == h for h in range(N_HEADS)]
    bd = bd_ref[...]
    p = (bd * (1.0 / HEAD_DIM)).astype(BF16)
    scores = [[lax.dot_general(jnp.where(mh, q[rows], jnp.zeros_like(q[rows])), k[rows], nt,
                               preferred_element_type=F32) for mh in masks] for rows in chunks]
    o = []
    for rows in chunks:
        o.append(jnp.dot(q[rows], state[...].astype(BF16), preferred_element_type=F32) * qd_ref[...])
        kdec = (k[rows].astype(F32) * kd_ref[...]).astype(BF16)
        upd = lax.dot_general(kdec, v[rows], (((0,), (0,)), ((), ())), preferred_element_type=F32)
        state[...] = state[...] * cd_ref[...] + bd * upd
    for c, rows in enumerate(chunks):
        for h, mh in enumerate(masks):
            oh = jnp.dot((scores[c][h] * dm_ref[h]).astype(BF16), v[rows], preferred_element_type=F32)
            o[c] = o[c] + jnp.where(mh, oh, 0.0)
    d = [oc - _split_dot(oc, p) for oc in o]
    var = [_split_dot(dc * dc, p) for dc in d]
    for c, rows in enumerate(chunks):
        on = d[c] * lax.rsqrt(var[c] + 1e-5) * gg_ref[...] + gb_ref[...]
        y_ref[rows, :] = (_silu(g[rows]) * on).astype(BF16)


def _in_proj_kernel(per_seq, x_ref, g_ref, w_ref, cr_ref, sr_ref, cm_ref, sm_ref,
                    cw_ref, cb_ref, lg_ref, lb_ref, wp_ref, pb_ref, ps_ref,
                    dm_ref, qd_ref, kd_ref, cd_ref, bd_ref, gg_ref, gb_ref,
                    ya_ref, yb_ref, mqt_ref, mk_ref, mvt_ref, km_ref, yd_ref,
                    hbuf, ubuf, pool_scr, state):
    G = GROUP
    ts = x_ref.shape[0]
    s = pl.program_id(0) % per_seq
    _shift_halo(hbuf, CONV_HALO, ts, s == 0)
    _shift_halo(ubuf, POOL_HALO, ts, s == 0)

    @pl.when(s == 0)
    def _():
        state[...] = jnp.zeros_like(state)

    hb = _rms(x_ref[...], g_ref[...]).astype(BF16)

    def proj(i, n=1):
        return jnp.dot(hb, w_ref[:, i * G:(i + n) * G], preferred_element_type=F32)

    uc = proj(0, 2)
    hbuf[CONV_HALO:, :] = uc[:, :G] * jax.nn.sigmoid(uc[:, G:])
    ubuf[POOL_HALO:, :] = proj(9)
    cr, sr = cr_ref[...], sr_ref[...]
    rq = _rotate_half(proj(2), cr, sr, HEAD_DIM // 2).astype(BF16)
    rk = (_rotate_half(proj(3), cr, sr, HEAD_DIM // 2) * (HEAD_DIM ** -0.5)).astype(BF16)
    rv = proj(4).astype(BF16)
    rg = proj(5)
    cm, sm = cm_ref[...], sm_ref[...]
    mq = _rotate_half(proj(6), cm, sm, ROT_DIM // 2)
    mk = _rotate_half(proj(7), cm, sm, ROT_DIM // 2)
    mv = proj(8)
    mk_ref[...] = mk.astype(BF16)
    nblk = mq.shape[0] // MOBA_BLOCK
    for j in range(nblk):
        rows = slice(j * MOBA_BLOCK, (j + 1) * MOBA_BLOCK)
        mqt_ref[j] = mq[rows].T
        vt = mv[rows].T.astype(BF16)
        for h in range(N_HEADS):
            r = h * VT_ROWS
            mvt_ref[j, r:r + HEAD_DIM, :] = vt[h * HEAD_DIM:(h + 1) * HEAD_DIM]
            mvt_ref[j, r + HEAD_DIM:r + VT_ROWS, :] = jnp.ones((VT_ROWS - HEAD_DIM, MOBA_BLOCK), BF16)
        km_ref[0, j:j + 1, :] = jnp.mean(mk[rows], axis=0, keepdims=True)

    _retention_tile(rq, rk, rv, rg, state, dm_ref, qd_ref, kd_ref, cd_ref, bd_ref, gg_ref, gb_ref, yb_ref)

    rc = LOCAL_CHUNK
    for r0 in range(0, ts, rc):
        ya_ref[r0:r0 + rc, :] = _conv_chunk(hbuf, r0, rc, cw_ref, cb_ref, lg_ref, lb_ref).astype(BF16)
        pool_scr[r0:r0 + rc, :] = _pool_chunk(ubuf, r0, rc, s * ts).astype(BF16)
    y = jnp.dot(pool_scr[...], wp_ref[...], preferred_element_type=F32)
    yd_ref[...] = ((y + pb_ref[...]) * ps_ref[...]).astype(BF16)


def _in_proj(x2, layer, g, w_in, tabs, local, rtabs, gn, seq):
    T, D = x2.shape
    tm = ROW_TILE
    nt = T // tm
    per_seq = seq // tm
    nblk = tm // MOBA_BLOCK
    G = GROUP
    row = lambda i: (i, 0)
    tab = pl.BlockSpec((tm, G), lambda i: (i % per_seq, 0))
    lay = lambda a: _layer_spec(a, layer)
    bfo = jax.ShapeDtypeStruct((T, G), BF16)
    blk3 = lambda rows, dt: jax.ShapeDtypeStruct((T // MOBA_BLOCK, rows, MOBA_BLOCK), dt)
    o_row = pl.BlockSpec((tm, G), row)
    o_blk = lambda rows: pl.BlockSpec((nblk, rows, MOBA_BLOCK), lambda i: (i, 0, 0))
    return pl.pallas_call(
        functools.partial(_in_proj_kernel, per_seq),
        grid=(nt,),
        in_specs=[pl.BlockSpec((tm, D), row), lay(g), lay(w_in), tab, tab, tab, tab,
                  *[lay(a) for a in local],
                  *[_const_spec(a.shape) for a in rtabs],
                  *[lay(a) for a in gn]],
        out_specs=[o_row, o_row, o_blk(G), o_row, o_blk(N_HEADS * VT_ROWS),
                   pl.BlockSpec((1, nblk, G), lambda i: (i, 0, 0)), o_row],
        out_shape=[bfo, bfo, blk3(G, F32), bfo, blk3(N_HEADS * VT_ROWS, BF16),
                   jax.ShapeDtypeStruct((nt, nblk, G), F32), bfo],
        scratch_shapes=[pltpu.VMEM((CONV_HALO + tm, G), F32), pltpu.VMEM((POOL_HALO + tm, G), F32),
                        pltpu.VMEM((tm, G), BF16), pltpu.VMEM((G, G), F32)],
        compiler_params=_cparams(("arbitrary",)),
        name="in_proj",
    )(x2, g, w_in, *tabs, *local, *rtabs, *gn)


def _dot3(a, b):
    ah, bh = a.astype(BF16), b.astype(BF16)
    al, bl = (a - ah.astype(F32)).astype(BF16), (b - bh.astype(F32)).astype(BF16)
    dot = functools.partial(jnp.dot, preferred_element_type=F32)
    return dot(ah, bh) + (dot(ah, bl) + dot(al, bh))


def _moba_kernel(qt_ref, k_ref, vt_ref, km_ref, y_ref, sel_ref, qh_ref, m_ref, acc_ref, ot_ref, sa_ref, sb_ref):
    a = pl.program_id(1)
    nb = km_ref.shape[1]
    Q = MOBA_BLOCK
    chains = [(t, h) for t in range(MOBA_TILES) for h in range(N_HEADS)]
    cid = lambda t, h: t * N_HEADS + h
    ch_head = _lane_head((GROUP, 1), 0)
    km = km_ref[0]
    km_head = _lane_head((1, GROUP), 1)
    blk = lax.broadcasted_iota(jnp.int32, (nb, Q), 0)

    km_heads = jnp.concatenate([jnp.where(km_head == h, km, 0.0) for h in range(N_HEADS)], axis=0)
    for t in range(MOBA_TILES):
        qt = qt_ref[t]
        qs = (qt * (HEAD_DIM ** -0.5 * LOG2E)).astype(BF16)
        gate_all = _dot3(km_heads, qt)
        past = blk < MOBA_TILES * a + t
        for h in range(N_HEADS):
            gate = jnp.where(past, gate_all[h * nb:(h + 1) * nb], -jnp.inf)
            sel = jnp.zeros((nb, Q), F32)
            for _ in range(MOBA_TOPK):
                top = jnp.max(gate, axis=0, keepdims=True)
                first = jnp.min(jnp.where(gate == top, blk, nb), axis=0, keepdims=True)
                pick = (blk == first) & past
                sel = jnp.where(pick, 1.0, sel)
                gate = jnp.where(pick, -jnp.inf, gate)
            sel_ref[cid(t, h) * nb:(cid(t, h) + 1) * nb, :] = sel
            qh_ref[cid(t, h)] = jnp.where(ch_head == h, qs, jnp.zeros_like(qs))

    m_ref[...] = jnp.full(m_ref.shape, -BIG, F32)
    acc_ref[...] = jnp.zeros(acc_ref.shape, F32)

    def score(j, s_ref, which):
        kb = k_ref[j]
        for t, h in which:
            s_ref[cid(t, h)] = jnp.dot(kb, qh_ref[cid(t, h)], preferred_element_type=F32)

    def absorb(j, s_ref, which, own_tile=None):
        causal = None
        for t, h in which:
            c = cid(t, h)
            m = m_ref[c:c + 1, :]
            s = s_ref[c]
            if t == own_tile:
                if causal is None:
                    causal = (lax.broadcasted_iota(jnp.int32, (Q, Q), 0)
                              <= lax.broadcasted_iota(jnp.int32, (Q, Q), 1))
                s = jnp.where(causal, s, NEG)
                m_new = jnp.maximum(m, jnp.max(s, axis=0, keepdims=True))
                shift = m_new
            else:
                on = sel_ref[pl.ds(c * nb + j, 1), :] > 0.0
                m_new = jnp.where(on, jnp.maximum(m, jnp.max(s, axis=0, keepdims=True)), m)
                shift = jnp.where(on, m_new, BIG)
            p = jnp.exp2(s - shift).astype(BF16)
            vt = vt_ref[j, h * VT_ROWS:(h + 1) * VT_ROWS, :]
            acc_ref[c] = jnp.exp2(m - m_new) * acc_ref[c] + jnp.dot(vt, p, preferred_element_type=F32)
            m_ref[c:c + 1, :] = m_new

    def body(i, carry):
        score(2 * i + 1, sb_ref, chains)
        absorb(2 * i, sa_ref, chains)
        score(2 * i + 2, sa_ref, chains)
        absorb(2 * i + 1, sb_ref, chains)
        return carry

    first = MOBA_TILES * a
    score(0, sa_ref, chains)
    lax.fori_loop(0, first // 2, body, 0)
    bufs = (sa_ref, sb_ref)
    for k in range(MOBA_TILES):
        if k + 1 < MOBA_TILES:
            score(first + k + 1, bufs[(k + 1) % 2], [c for c in chains if c[0] > k])
        absorb(first + k, bufs[k % 2], [c for c in chains if c[0] >= k], own_tile=k)

    for t in range(MOBA_TILES):
        for h in range(N_HEADS):
            acc = acc_ref[cid(t, h)]
            ot_ref[h * HEAD_DIM:(h + 1) * HEAD_DIM, :] = acc[:HEAD_DIM] / acc[HEAD_DIM:HEAD_DIM + 1]
        y_ref[t * Q:(t + 1) * Q, :] = ot_ref[...].T.astype(BF16)


def _moba(mqt, mk3, mvt, km, batch, seq):
    nb = seq // MOBA_BLOCK
    B = MOBA_BLOCK
    G = GROUP
    nt = MOBA_TILES
    assert nt % 2 == 0 and nb % nt == 0
    steps = nb // nt
    nc = nt * N_HEADS
    return pl.pallas_call(
        _moba_kernel,
        grid=(batch, steps),
        in_specs=[pl.BlockSpec((nt, G, B), lambda b, i: (b * steps + i, 0, 0)),
                  pl.BlockSpec((nb, B, G), lambda b, i: (b, 0, 0)),
                  pl.BlockSpec((nb, N_HEADS * VT_ROWS, B), lambda b, i: (b, 0, 0)),
                  pl.BlockSpec((1, nb, G), lambda b, i: (b, 0, 0))],
        out_specs=pl.BlockSpec((nt * B, G), lambda b, i: (b * steps + i, 0)),
        out_shape=jax.ShapeDtypeStruct((batch * seq, G), BF16),
        scratch_shapes=[pltpu.VMEM((nc * nb, B), F32),
                        pltpu.VMEM((nc, G, B), BF16),
                        pltpu.VMEM((nc, B), F32),
                        pltpu.VMEM((nc, VT_ROWS, B), F32),
                        pltpu.VMEM((G, B), F32),
                        pltpu.VMEM((nc, B, B), F32),
                        pltpu.VMEM((nc, B, B), F32)],
        compiler_params=_cparams(("arbitrary", "arbitrary")),
        name="moba",
    )(mqt, mk3, mvt, km)


def _post_kernel(x_ref, ya_ref, yb_ref, yc_ref, yd_ref, wo_ref, gpo_ref, gfi_ref,
                 wg_ref, wu_ref, wd_ref, gfo_ref, o_ref, f_scr):
    G = GROUP
    tm = x_ref.shape[0]
    d_ff = wg_ref.shape[1]
    halves = [slice(r, r + tm // POST_SPLIT) for r in range(0, tm, tm // POST_SPLIT)]
    ys = []
    for rows in halves:
        y = jnp.dot(ya_ref[rows, :], wo_ref[0:G, :], preferred_element_type=F32)
        y = y + jnp.dot(yb_ref[rows, :], wo_ref[G:2 * G, :], preferred_element_type=F32)
        y = y + jnp.dot(yc_ref[rows, :], wo_ref[2 * G:3 * G, :], preferred_element_type=F32)
        ys.append(y + jnp.dot(yd_ref[rows, :], wo_ref[3 * G:4 * G, :], preferred_element_type=F32))
    for rows, y in zip(halves, ys):
        x1 = x_ref[rows, :] + _rms(y, gpo_ref[...])
        hb = _rms(x1, gfi_ref[...]).astype(BF16)
        for c in range(0, d_ff, FF_CHUNK):
            cols = slice(c, c + FF_CHUNK)
            gt = jnp.dot(hb, wg_ref[:, cols], preferred_element_type=F32)
            up = jnp.dot(hb, wu_ref[:, cols], preferred_element_type=F32)
            f_scr[rows, cols] = (_silu(gt) * up).astype(BF16)
        f = jnp.dot(f_scr[rows, :], wd_ref[...], preferred_element_type=F32)
        o_ref[rows, :] = x1 + _rms(f, gfo_ref[...])


def _post(x2, layer, ya, yb, yc, yd, params):
    T, D = x2.shape
    tm = POST_TILE
    G = GROUP
    d_ff = params[3].shape[-1]
    row = lambda i: (i, 0)
    xs = pl.BlockSpec((tm, D), row)
    ys = pl.BlockSpec((tm, G), row)
    return pl.pallas_call(
        _post_kernel,
        grid=(T // tm,),
        in_specs=[xs, ys, ys, ys, ys, *[_layer_spec(a, layer) for a in params]],
        out_specs=xs,
        out_shape=jax.ShapeDtypeStruct((T, D), F32),
        scratch_shapes=[pltpu.VMEM((tm, d_ff), BF16)],
        compiler_params=_cparams(("arbitrary",)),
        name="out_proj_ffn",
    )(x2, ya, yb, yc, yd, *params)


def _rotary_tables(seq, rot_dim, theta):
    pos = jnp.arange(seq, dtype=F32)
    inv = 1.0 / (theta ** (jnp.arange(0, rot_dim, 2, dtype=F32) / rot_dim))
    ang = pos[:, None] * inv[None, :]
    cos, sin = jnp.cos(ang), jnp.sin(ang)
    rest = HEAD_DIM - rot_dim
    cos_h = jnp.concatenate([cos, cos, jnp.ones((seq, rest), F32)], axis=-1)
    sin_h = jnp.concatenate([-sin, sin, jnp.zeros((seq, rest), F32)], axis=-1)
    return jnp.tile(cos_h, (1, N_HEADS)), jnp.tile(sin_h, (1, N_HEADS))


def _retention_tables():
    c = RET_CHUNK
    log_gamma = jnp.log(1.0 - 2.0 ** (-5.0 - jnp.arange(N_HEADS, dtype=F32)))
    idx = jnp.arange(c, dtype=F32)
    diff = idx[:, None] - idx[None, :]
    dm = jnp.where(diff[None] >= 0, jnp.exp(jnp.maximum(diff, 0.0)[None] * log_gamma[:, None, None]), 0.0)
    lanes = lambda t: jnp.repeat(t, HEAD_DIM, axis=-1)
    qd = lanes(jnp.exp((idx + 1.0)[:, None] * log_gamma[None, :]))
    kd = lanes(jnp.exp((c - 1.0 - idx)[:, None] * log_gamma[None, :]))
    cd = lanes(jnp.exp(c * log_gamma)[None, :])
    head = jnp.arange(GROUP) // HEAD_DIM
    bd = (head[:, None] == head[None, :]).astype(F32)
    return dm, qd, kd, cd, bd


def _block_diag(w):
    n, c, _ = w.shape
    eye = jnp.eye(n, dtype=w.dtype)
    return (eye[:, None, :, None] * w[:, :, None, :]).reshape(n * c, n * c)


def kernel(x, attn_pre_g, w_in, conv_w, conv_b, conv_ln_g, conv_ln_b, ret_gn_g, ret_gn_b, pool_w, pool_b,
           pool_scale, w_out, attn_post_g, ffn_pre_g, w_gate, w_up, w_down, ffn_post_g):
    batch, seq, d_model = x.shape
    depth = w_in.shape[0]
    assert seq % ROW_TILE == 0 and ROW_TILE % LOCAL_CHUNK == 0 and seq % RET_CHUNK == 0 and seq % MOBA_BLOCK == 0
    assert (batch * seq) % POST_TILE == 0
    nb = seq // MOBA_BLOCK
    tabs = _rotary_tables(seq, HEAD_DIM, RET_ROT_THETA) + _rotary_tables(seq, ROT_DIM, ROPE_THETA)
    rtabs = _retention_tables()
    vec = lambda a: a.reshape(depth, 1, -1)
    bf = lambda a: a.astype(BF16)
    in_g, in_w = vec(attn_pre_g), bf(w_in)
    local = (conv_w, vec(conv_b), vec(conv_ln_g), vec(conv_ln_b),
             bf(jax.vmap(_block_diag)(pool_w)), vec(pool_b), vec(pool_scale))
    gn = (vec(ret_gn_g), vec(ret_gn_b))
    post = (bf(w_out), vec(attn_post_g), vec(ffn_pre_g), bf(w_gate), bf(w_up), bf(w_down), vec(ffn_post_g))

    x2 = x.reshape(batch * seq, d_model)
    for l in range(depth):
        ya, yb, mqt, mk, mvt, km, yd = _in_proj(x2, l, in_g, in_w, tabs, local, rtabs, gn, seq)
        yc = _moba(mqt, mk.reshape(batch * nb, MOBA_BLOCK, GROUP), mvt, km.reshape(batch, nb, GROUP), batch, seq)
        x2 = _post(x2, l, ya, yb, yc, yd, post)
    return x2.reshape(batch, seq, d_model)
```
